```python
import math, functools
import jax, jax.numpy as jnp
from jax import lax
import numpy as np

D_MODEL = 2048
BATCH = 1
SEQ = 8192
DEPTH = 2
DEC_BATCH = 32
DEC_SEQ = 4
PAST_LEN = 8192
PAGE_SIZE = 128

N_META = 16
N_MIXERS = 2
Q_BLOCK = 128
KV_DIM = D_MODEL
DIFF_HEADS = D_MODEL // 128
DIFF_DK = 64
DIFF_DV = 2 * DIFF_DK
SB_HEADS = D_MODEL // 128
SB_DH = 128
ROPE_THETA = 10000.0
N_EXPERTS = 64
TOP_K = 8
N_GROUPS = 8
TOPK_GROUPS = 4
MOE_FF = D_MODEL // 4
SHARED_FF = D_MODEL // 4
ROUTED_SCALE = 2.5
EXPERT_BLOCK = 64
ALPHA = (2 * DEPTH) ** 0.25
BETA = (8 * DEPTH) ** -0.25
N_DIFF_LAYERS = (DEPTH + 1) // 2
LN_EPS = 1e-5
NEG_INF = -1e30

kernel_name = "diff_stickbreak_moe_decoder_step"


def layer_norm(x, g, b):
    xf = x.astype(jnp.float32)
    mu = xf.mean(-1, keepdims=True)
    var = jnp.square(xf - mu).mean(-1, keepdims=True)
    y = (xf - mu) * lax.rsqrt(var + LN_EPS) * g.astype(jnp.float32) + b.astype(jnp.float32)
    return y.astype(x.dtype)


def rms_norm(x, g):
    xf = x.astype(jnp.float32)
    y = xf * lax.rsqrt(jnp.mean(xf * xf, -1, keepdims=True) + LN_EPS) * g.astype(jnp.float32)
    return y.astype(x.dtype)


def rope(x, pos):
    half = DIFF_DK // 2
    inv = ROPE_THETA ** (-jnp.arange(half, dtype=jnp.float32) / half)
    ang = pos.astype(jnp.float32)[:, None] * inv[None, :]
    shape = (1, pos.shape[0]) + (1,) * (x.ndim - 3) + (DIFF_DK,)
    cos = jnp.concatenate([jnp.cos(ang), jnp.cos(ang)], -1).reshape(shape).astype(x.dtype)
    sin = jnp.concatenate([jnp.sin(ang), jnp.sin(ang)], -1).reshape(shape).astype(x.dtype)
    rot = jnp.concatenate([-x[..., half:], x[..., :half]], -1)
    return x * cos + rot * sin


def diff_attend(q, k, v, q_pos, k_pos, lam):
    s = jnp.einsum('bqhcd,bshcd->bhcqs', q, k).astype(jnp.float32) * (DIFF_DK ** -0.5)
    mask = k_pos[None, :] <= q_pos[:, None]
    p = jax.nn.softmax(jnp.where(mask, s, NEG_INF), axis=-1)
    a = p[:, :, 0] - lam * p[:, :, 1]
    return jnp.einsum('bhqs,bshe->bqhe', a.astype(v.dtype), v)


def sb_attend(q, k, v, q_pos, k_pos):
    z = jnp.einsum('bqhd,bshd->bhqs', q, k).astype(jnp.float32) * (SB_DH ** -0.5)
    mask = k_pos[None, :] < q_pos[:, None]
    log_keep = jnp.where(mask, jax.nn.log_sigmoid(-z), 0.0)
    later = lax.cumsum(log_keep, axis=3, reverse=True) - log_keep
    w = jnp.where(mask, jnp.exp(jax.nn.log_sigmoid(z) + later), 0.0)
    return jnp.einsum('bhqs,bshd->bqhd', w.astype(v.dtype), v)


def attend_causal(fn, q, k, v, q_pos, k_pos, blocked):
    if not blocked:
        return fn(q, k, v, q_pos, k_pos)
    b, L = q.shape[:2]
    nb = -(-L // Q_BLOCK)
    pad = nb * Q_BLOCK - L
    padl = lambda a: jnp.pad(a, [(0, 0), (0, pad)] + [(0, 0)] * (a.ndim - 2))
    qp, kp, vp = padl(q), padl(k), padl(v)
    pos = jnp.arange(nb * Q_BLOCK, dtype=jnp.int32)
    qb = jnp.swapaxes(qp.reshape((b, nb, Q_BLOCK) + q.shape[2:]), 0, 1)
    ob = lax.map(lambda a: fn(a[0], kp, vp, a[1], pos), (qb, pos.reshape(nb, Q_BLOCK)))
    o = jnp.swapaxes(ob, 0, 1)
    return o.reshape((b, nb * Q_BLOCK) + o.shape[3:])[:, :L]


def with_past(k_new, v_new, pos, k_past, v_past):
    if k_past is None:
        return k_new, v_new, pos
    k_pos = jnp.arange(k_past.shape[1] + k_new.shape[1], dtype=jnp.int32)
    return jnp.concatenate([k_past, k_new], 1), jnp.concatenate([v_past, v_new], 1), k_pos


def diff_mixer(x, pos, k_past, v_past, *, w_qkv, w_o, lam_vec, gain, lam_init, blocked):
    B, L, _ = x.shape
    qkv = x @ w_qkv
    q = rope(qkv[..., :KV_DIM].reshape(B, L, DIFF_HEADS, 2, DIFF_DK), pos)
    k_new = rope(qkv[..., KV_DIM:2 * KV_DIM].reshape(B, L, DIFF_HEADS, 2, DIFF_DK), pos).reshape(B, L, KV_DIM)
    v_new = qkv[..., 2 * KV_DIM:]
    k_all, v_all, k_pos = with_past(k_new, v_new, pos, k_past, v_past)
    lq1, lk1, lq2, lk2 = lam_vec.astype(jnp.float32)
    lam = jnp.exp(jnp.sum(lq1 * lk1)) - jnp.exp(jnp.sum(lq2 * lk2)) + lam_init
    o = attend_causal(functools.partial(diff_attend, lam=lam), q,
                      k_all.reshape(B, -1, DIFF_HEADS, 2, DIFF_DK),
                      v_all.reshape(B, -1, DIFF_HEADS, DIFF_DV), pos, k_pos, blocked)
    o = rms_norm(o, gain) * (1.0 - lam_init)
    return o.reshape(B, L, D_MODEL) @ w_o, k_new, v_new


def sb_mixer(x, pos, k_past, v_past, *, w_qkv, w_o, blocked):
    B, L, _ = x.shape
    qkv = x @ w_qkv
    q = qkv[..., :KV_DIM].reshape(B, L, SB_HEADS, SB_DH)
    k_new = qkv[..., KV_DIM:2 * KV_DIM]
    v_new = qkv[..., 2 * KV_DIM:]
    k_all, v_all, k_pos = with_past(k_new, v_new, pos, k_past, v_past)
    o = attend_causal(sb_attend, q, k_all.reshape(B, -1, SB_HEADS, SB_DH),
                      v_all.reshape(B, -1, SB_HEADS, SB_DH), pos, k_pos, blocked)
    return o.reshape(B, L, D_MODEL) @ w_o, k_new, v_new


def swiglu(x, wg, wu, wd):
    return (jax.nn.silu(x @ wg) * (x @ wu)) @ wd


def routed_experts(x, eidx, gate, wg, wu, wd):
    T, D = x.shape
    n = T * TOP_K
    nb = -(-n // EXPERT_BLOCK) + N_EXPERTS
    e_flat = eidx.reshape(n)
    tok = jnp.arange(n, dtype=jnp.int32) // TOP_K
    order = jnp.argsort(e_flat)
    e_s, tok_s, g_s = e_flat[order], tok[order], gate.reshape(n)[order]
    counts = jnp.bincount(e_flat, length=N_EXPERTS)
    blocks = (counts + EXPERT_BLOCK - 1) // EXPERT_BLOCK
    blk_end = jnp.cumsum(blocks)
    blk_start = blk_end - blocks
    sorted_start = jnp.cumsum(counts) - counts
    dest = blk_start[e_s] * EXPERT_BLOCK + (jnp.arange(n, dtype=jnp.int32) - sorted_start[e_s])
    slot_tok = jnp.full((nb * EXPERT_BLOCK,), T, jnp.int32).at[dest].set(tok_s)
    slot_gate = jnp.zeros((nb * EXPERT_BLOCK,), jnp.float32).at[dest].set(g_s)
    block_expert = jnp.minimum(jnp.searchsorted(blk_end, jnp.arange(nb, dtype=blk_end.dtype), side='right'),
                               N_EXPERTS - 1)
    xb = jnp.concatenate([x, jnp.zeros((1, D), x.dtype)], 0)[slot_tok].reshape(nb, EXPERT_BLOCK, D)
    yb = lax.map(lambda a: swiglu(a[0], wg[a[1]], wu[a[1]], wd[a[1]]), (xb, block_expert))
    contrib = yb.reshape(nb * EXPERT_BLOCK, D).astype(jnp.float32) * slot_gate[:, None]
    return jax.ops.segment_sum(contrib, slot_tok, num_segments=T + 1)[:T].astype(x.dtype)


def moe_ffn(x, *, w_r, b_r, wg, wu, wd, sg, su, sd):
    B, L, D = x.shape
    x2 = x.reshape(B * L, D)
    scores = jax.nn.sigmoid(x2.astype(jnp.float32) @ w_r.astype(jnp.float32))
    biased = scores + b_r.astype(jnp.float32)
    grp = lax.top_k(biased.reshape(B * L, N_GROUPS, N_EXPERTS // N_GROUPS), 2)[0].sum(-1)
    _, gidx = lax.top_k(grp, TOPK_GROUPS)
    gmask = jax.nn.one_hot(gidx, N_GROUPS, dtype=jnp.float32).sum(1) > 0
    emask = jnp.repeat(gmask, N_EXPERTS // N_GROUPS, axis=1)
    _, eidx = lax.top_k(jnp.where(emask, biased, -jnp.inf), TOP_K)
    sel = jnp.take_along_axis(scores, eidx, axis=1)
    gate = sel / sel.sum(-1, keepdims=True) * ROUTED_SCALE
    y = swiglu(x2, sg, su, sd) + routed_experts(x2, eidx, gate, wg, wu, wd)
    return y.reshape(B, L, D)


def setup_inputs(seed: int = 0) -> dict:
    key = jax.random.key(seed)
    ks = jax.random.split(key, 24)
    D = D_MODEL
    n_pages = PAST_LEN // PAGE_SIZE
    n_used = DEC_BATCH * n_pages
    n_phys = n_used + max(1, n_used // 4)
    nrm = lambda k, shape, scale: jax.random.normal(k, shape, jnp.float32) * scale
    page_table = jax.random.permutation(ks[4], n_phys)[:n_used].reshape(DEC_BATCH, n_pages).astype(jnp.int32)
    return {
        "x_prompt": nrm(ks[0], (BATCH, SEQ, D), 1.0),
        "x_sample": nrm(ks[1], (DEC_BATCH, DEC_SEQ, D), 1.0),
        "cache_k": nrm(ks[2], (DEPTH, n_phys, PAGE_SIZE, KV_DIM), 1.0),
        "cache_v": nrm(ks[3], (DEPTH, n_phys, PAGE_SIZE, KV_DIM), 1.0),
        "page_table": page_table,
        "meta_tokens": nrm(ks[5], (N_META, D), 1.0),
        "attn_w_qkv": nrm(ks[6], (DEPTH, D, 3 * KV_DIM), D ** -0.5),
        "attn_w_o": nrm(ks[7], (DEPTH, D, D), BETA * D ** -0.5),
        "diff_lambda": nrm(ks[8], (N_DIFF_LAYERS, 4, DIFF_DK), 0.1),
        "diff_subln": 1.0 + nrm(ks[9], (N_DIFF_LAYERS, DIFF_DV), 0.02),
        "ln_mix_g": 1.0 + nrm(ks[10], (DEPTH, D), 0.02),
        "ln_mix_b": nrm(ks[11], (DEPTH, D), 0.02),
        "router_w": nrm(ks[12], (DEPTH, D, N_EXPERTS), D ** -0.5),
        "router_bias": nrm(ks[13], (DEPTH, N_EXPERTS), 0.01),
        "expert_w_gate": nrm(ks[14], (DEPTH, N_EXPERTS, D, MOE_FF), D ** -0.5),
        "expert_w_up": nrm(ks[15], (DEPTH, N_EXPERTS, D, MOE_FF), D ** -0.5),
        "expert_w_down": nrm(ks[16], (DEPTH, N_EXPERTS, MOE_FF, D), BETA * MOE_FF ** -0.5),
        "shared_w_gate": nrm(ks[17], (DEPTH, D, SHARED_FF), D ** -0.5),
        "shared_w_up": nrm(ks[18], (DEPTH, D, SHARED_FF), D ** -0.5),
        "shared_w_down": nrm(ks[19], (DEPTH, SHARED_FF, D), BETA * SHARED_FF ** -0.5),
        "ln_ffn_g": 1.0 + nrm(ks[20], (DEPTH, D), 0.02),
        "ln_ffn_b": nrm(ks[21], (DEPTH, D), 0.02),
    }


def reference(x_prompt, x_sample, cache_k, cache_v, page_table, meta_tokens, attn_w_qkv, attn_w_o,
              diff_lambda, diff_subln, ln_mix_g, ln_mix_b, router_w, router_bias, expert_w_gate,
              expert_w_up, expert_w_down, shared_w_gate, shared_w_up, shared_w_down, ln_ffn_g, ln_ffn_b):
    bp = x_prompt.shape[0]
    meta = jnp.broadcast_to(meta_tokens.astype(x_prompt.dtype)[None], (bp, N_META, D_MODEL))
    xp = jnp.concatenate([meta, x_prompt], axis=1)
    xs = x_sample
    db, ds = xs.shape[:2]
    past_len = page_table.shape[1] * cache_k.shape[2]
    pos_p = jnp.arange(xp.shape[1], dtype=jnp.int32)
    pos_s = past_len + jnp.arange(ds, dtype=jnp.int32)
    kp_rows, vp_rows, ks_rows, vs_rows = [], [], [], []
    for i in range(DEPTH):
        k_past = cache_k[i, page_table].reshape(db, past_len, KV_DIM)
        v_past = cache_v[i, page_table].reshape(db, past_len, KV_DIM)
        if i % N_MIXERS == 0:
            j = i // N_MIXERS
            lam_init = 0.8 - 0.6 * math.exp(-0.3 * i)
            mix = functools.partial(diff_mixer, w_qkv=attn_w_qkv[i], w_o=attn_w_o[i], lam_vec=diff_lambda[j],
                                    gain=diff_subln[j], lam_init=lam_init)
        else:
            mix = functools.partial(sb_mixer, w_qkv=attn_w_qkv[i], w_o=attn_w_o[i])
        hp, k_np, v_np = mix(xp, pos_p, None, None, blocked=True)
        hs, k_ns, v_ns = mix(xs, pos_s, k_past, v_past, blocked=False)
        kp_rows.append(k_np)
        vp_rows.append(v_np)
        ks_rows.append(k_ns)
        vs_rows.append(v_ns)
        xp = layer_norm(ALPHA * xp + hp, ln_mix_g[i], ln_mix_b[i])
        xs = layer_norm(ALPHA * xs + hs, ln_mix_g[i], ln_mix_b[i])
        ffn = functools.partial(moe_ffn, w_r=router_w[i], b_r=router_bias[i], wg=expert_w_gate[i],
                                wu=expert_w_up[i], wd=expert_w_down[i], sg=shared_w_gate[i],
                                su=shared_w_up[i], sd=shared_w_down[i])
        xp = layer_norm(ALPHA * xp + ffn(xp), ln_ffn_g[i], ln_ffn_b[i])
        xs = layer_norm(ALPHA * xs + ffn(xs), ln_ffn_g[i], ln_ffn_b[i])
    y_prompt = xp[:, N_META:]
    return (y_prompt, xs, jnp.stack(kp_rows), jnp.stack(vp_rows), jnp.stack(ks_rows), jnp.stack(vs_rows))
```

```python
import functools
import math

import jax
import jax.numpy as jnp
from jax import lax
from jax.experimental import pallas as pl
from jax.experimental.pallas import tpu as pltpu

F32 = jnp.float32
BF16 = jnp.bfloat16

N_META = 16
HEAD_DIM = 128
DIFF_DK = 64
ROPE_THETA = 10000.0
N_EXPERTS = 64
N_GROUPS = 8
GROUP_SIZE = N_EXPERTS // N_GROUPS
TOPK_GROUPS = 4
TOP_K = 8
ROUTED_SCALE = 2.5
LN_EPS = 1e-5
NEG_INF = -1e30

LANES = 128
ROW_TILE = 256
EXPERT_ROWS = 256
DECODE_ROWS_PER_HEAD = 16
VMEM_LIMIT_BYTES = 56 * 1024 * 1024

_NT = (((1,), (1,)), ((), ()))


def _params(*sem):
    return pltpu.CompilerParams(dimension_semantics=sem, vmem_limit_bytes=VMEM_LIMIT_BYTES)


def _softplus(z):
    return jnp.maximum(z, 0.0) + jnp.log(1.0 + jnp.exp(-jnp.abs(z)))


def _layer_norm(y, g, b):
    mu = jnp.mean(y, axis=1, keepdims=True)
    d = y - mu
    var = jnp.mean(d * d, axis=1, keepdims=True)
    return d * lax.rsqrt(var + LN_EPS) * g + b


def _split_bf16(x):
    hi = x.astype(BF16)
    lo = (x - hi.astype(F32)).astype(BF16)
    return hi, lo


def _suffix_sum(lk, upper):
    hi, lo = _split_bf16(lk)
    return (jnp.dot(hi, upper, preferred_element_type=F32)
            + jnp.dot(lo, upper, preferred_element_type=F32))


def _strict_lower(n):
    r = lax.broadcasted_iota(jnp.int32, (n, n), 0)
    c = lax.broadcasted_iota(jnp.int32, (n, n), 1)
    return jnp.where(r > c, 1.0, 0.0).astype(BF16)


def _diff_lambda(lam_ref, lam_init):
    lv = lam_ref[...]
    a = jnp.sum(lv[0:1] * lv[1:2], axis=(0, 1), keepdims=True)
    b = jnp.sum(lv[2:3] * lv[3:4], axis=(0, 1), keepdims=True)
    return jnp.exp(a) - jnp.exp(b) + lam_init


def _sub_rms(o, gain, lam_init):
    ms = jnp.mean(o * o, axis=1, keepdims=True)
    return o * lax.rsqrt(ms + LN_EPS) * gain * (1.0 - lam_init)


def _rope(val, cos, sin):
    lane = lax.broadcasted_iota(jnp.int32, cos.shape, 1)
    first_half = (lane % DIFF_DK) < (DIFF_DK // 2)
    outs = []
    for s in range(val.shape[1] // LANES):
        xs = val[:, s * LANES:(s + 1) * LANES]
        rot = jnp.where(first_half, pltpu.roll(xs, LANES - DIFF_DK // 2, 1), pltpu.roll(xs, DIFF_DK // 2, 1))
        outs.append(xs * cos + rot * sin)
    return jnp.concatenate(outs, axis=1)


def _qkv_kernel(x_ref, w_ref, cos_ref, sin_ref, o32_ref, o16_ref, *, rope, q_scale):
    j = pl.program_id(0)
    acc = jnp.dot(x_ref[...], w_ref[...], preferred_element_type=F32)
    scale = jnp.where(j == 0, q_scale, 1.0).astype(F32)

    def store(val):
        o32_ref[...] = val
        o16_ref[...] = (val * scale).astype(BF16)

    if rope:
        @pl.when(j < 2)
        def _():
            store(_rope(acc, cos_ref[...], sin_ref[...]))

        @pl.when(j == 2)
        def _():
            store(acc)
    else:
        store(acc)


def _qkv_proj(x16, w3, cos, sin, *, rope, q_scale):
    tp, d = x16.shape
    tm = ROW_TILE
    return pl.pallas_call(
        functools.partial(_qkv_kernel, rope=rope, q_scale=q_scale),
        grid=(3, tp // tm),
        in_specs=[
            pl.BlockSpec((tm, d), lambda j, i: (i, 0)),
            pl.BlockSpec((None, d, d), lambda j, i: (j, 0, 0)),
            pl.BlockSpec((tm, LANES), lambda j, i: (i, 0)),
            pl.BlockSpec((tm, LANES), lambda j, i: (i, 0)),
        ],
        out_specs=[
            pl.BlockSpec((None, tm, d), lambda j, i: (j, i, 0)),
            pl.BlockSpec((None, tm, d), lambda j, i: (j, i, 0)),
        ],
        out_shape=[jax.ShapeDtypeStruct((3, tp, d), F32), jax.ShapeDtypeStruct((3, tp, d), BF16)],
        compiler_params=_params("arbitrary", "arbitrary"),
        name="qkv_proj",
    )(x16, w3, cos, sin)


def _diff_attn_kernel(q_ref, k_ref, v_ref, lam_ref, gain_ref, o_ref, m_sc, l_sc, acc_sc, *, tq, lam_init):
    qi = pl.program_id(1)
    q = q_ref[...]
    lane = lax.broadcasted_iota(jnp.int32, q.shape, 1)
    zero = jnp.zeros_like(q)
    q2 = jnp.concatenate([jnp.where(lane < DIFF_DK, q, zero), jnp.where(lane >= DIFF_DK, q, zero)], axis=0)
    m_sc[...] = jnp.full(m_sc.shape, NEG_INF, F32)
    l_sc[...] = jnp.zeros(l_sc.shape, F32)
    acc_sc[...] = jnp.zeros(acc_sc.shape, F32)

    def block(j, masked):
        start = pl.multiple_of(j * tq, tq)
        kb = k_ref[pl.ds(start, tq), :]
        vb = v_ref[pl.ds(start, tq), :]
        s = lax.dot_general(q2, kb, _NT, preferred_element_type=F32)
        if masked:
            row = lax.broadcasted_iota(jnp.int32, s.shape, 0)
            row = jnp.where(row >= tq, row - tq, row)
            col = lax.broadcasted_iota(jnp.int32, s.shape, 1)
            s = jnp.where(col <= row, s, NEG_INF)
        m_prev = m_sc[...]
        m_new = jnp.maximum(m_prev, jnp.max(s, axis=1, keepdims=True))
        alpha = jnp.exp(m_prev - m_new)
        p = jnp.exp(s - m_new)
        l_sc[...] = alpha * l_sc[...] + jnp.sum(p, axis=1, keepdims=True)
        acc_sc[...] = alpha * acc_sc[...] + jnp.dot(p.astype(BF16), vb, preferred_element_type=F32)
        m_sc[...] = m_new

    def body(j, c):
        block(j, False)
        return c

    lax.fori_loop(0, qi, body, 0)
    block(qi, True)

    o = acc_sc[...] / l_sc[...]
    lam = _diff_lambda(lam_ref, lam_init)
    od = o[:tq] - lam * o[tq:]
    o_ref[...] = _sub_rms(od, gain_ref[...], lam_init).astype(BF16)


def _sb_attn_kernel(q_ref, k_ref, v_ref, o_ref, r_sc, acc_sc, *, tq):
    qi = pl.program_id(1)
    q = q_ref[...]
    upper = _strict_lower(tq)
    r_sc[...] = jnp.zeros(r_sc.shape, F32)
    acc_sc[...] = jnp.zeros(acc_sc.shape, F32)

    def block(j, masked):
        start = pl.multiple_of(j * tq, tq)
        kb = k_ref[pl.ds(start, tq), :]
        vb = v_ref[pl.ds(start, tq), :]
        z = lax.dot_general(q, kb, _NT, preferred_element_type=F32)
        sp = _softplus(z)
        lk = -sp
        if masked:
            row = lax.broadcasted_iota(jnp.int32, z.shape, 0)
            col = lax.broadcasted_iota(jnp.int32, z.shape, 1)
            valid = col < row
            lk = jnp.where(valid, lk, 0.0)
        r_prev = r_sc[...]
        w = jnp.exp(z - sp + _suffix_sum(lk, upper) + r_prev)
        if masked:
            w = jnp.where(valid, w, 0.0)
        acc_sc[...] += jnp.dot(w.astype(BF16), vb, preferred_element_type=F32)
        r_sc[...] = r_prev + jnp.sum(lk, axis=1, keepdims=True)

    block(qi, True)

    def body(it, c):
        block(qi - 1 - it, False)
        return c

    lax.fori_loop(0, qi, body, 0)
    o_ref[...] = acc_sc[...].astype(BF16)


def _prompt_attention(qkv16, lam_vec, gain, *, diff, lam_init):
    _, tp, d = qkv16.shape
    tq = ROW_TILE
    heads = d // HEAD_DIM
    qspec = pl.BlockSpec((None, tq, HEAD_DIM), lambda h, i: (0, i, h))
    kspec = pl.BlockSpec((None, tp, HEAD_DIM), lambda h, i: (1, 0, h))
    vspec = pl.BlockSpec((None, tp, HEAD_DIM), lambda h, i: (2, 0, h))
    ospec = pl.BlockSpec((tq, HEAD_DIM), lambda h, i: (i, h))
    out_shape = jax.ShapeDtypeStruct((tp, d), BF16)
    if diff:
        return pl.pallas_call(
            functools.partial(_diff_attn_kernel, tq=tq, lam_init=lam_init),
            grid=(heads, tp // tq),
            in_specs=[qspec, kspec, vspec,
                      pl.BlockSpec(lam_vec.shape, lambda h, i: (0, 0)),
                      pl.BlockSpec(gain.shape, lambda h, i: (0, 0))],
            out_specs=ospec,
            out_shape=out_shape,
            scratch_shapes=[pltpu.VMEM((2 * tq, 1), F32), pltpu.VMEM((2 * tq, 1), F32),
                            pltpu.VMEM((2 * tq, HEAD_DIM), F32)],
            compiler_params=_params("arbitrary", "arbitrary"),
            name="diff_attention",
        )(qkv16, qkv16, qkv16, lam_vec, gain)
    return pl.pallas_call(
        functools.partial(_sb_attn_kernel, tq=tq),
        grid=(heads, tp // tq),
        in_specs=[qspec, kspec, vspec],
        out_specs=ospec,
        out_shape=out_shape,
        scratch_shapes=[pltpu.VMEM((tq, 1), F32), pltpu.VMEM((tq, HEAD_DIM), F32)],
        compiler_params=_params("arbitrary", "arbitrary"),
        name="sb_attention",
    )(qkv16, qkv16, qkv16)


def _decode_kernel(pt_ref, qb_ref, kn_ref, vn_ref, kc_ref, vc_ref, lam_ref, gain_ref, o_ref,
                   m_sc, l_sc, acc_sc, *, diff, n_pages, dec_seq, lam_init, heads):
    s = pl.program_id(1)
    rph = DECODE_ROWS_PER_HEAD
    page = kc_ref.shape[0]

    @pl.when(s == 0)
    def _():
        m_sc[...] = jnp.full(m_sc.shape, NEG_INF if diff else 0.0, F32)
        l_sc[...] = jnp.zeros(l_sc.shape, F32)
        acc_sc[...] = jnp.zeros(acc_sc.shape, F32)

    def step(k_ref, v_ref, is_new):
        kp = k_ref[...].astype(BF16)
        sc = lax.dot_general(qb_ref[...], kp, _NT, preferred_element_type=F32)
        if is_new:
            row = lax.broadcasted_iota(jnp.int32, sc.shape, 0)
            col = lax.broadcasted_iota(jnp.int32, sc.shape, 1)
            qidx = row % dec_seq
        if diff:
            if is_new:
                sc = jnp.where((col <= qidx) & (col < dec_seq), sc, NEG_INF)
            m_prev = m_sc[...]
            m_new = jnp.maximum(m_prev, jnp.max(sc, axis=1, keepdims=True))
            alpha = jnp.exp(m_prev - m_new)
            w = jnp.exp(sc - m_new)
            l_sc[...] = alpha * l_sc[...] + jnp.sum(w, axis=1, keepdims=True)
            m_sc[...] = m_new
        else:
            sp = _softplus(sc)
            lk = -sp
            if is_new:
                valid = col < qidx
                lk = jnp.where(valid, lk, 0.0)
            r_prev = m_sc[...]
            w = jnp.exp(sc - sp + _suffix_sum(lk, _strict_lower(page)) + r_prev)
            if is_new:
                w = jnp.where(valid, w, 0.0)
            m_sc[...] = r_prev + jnp.sum(lk, axis=1, keepdims=True)
        w16 = w.astype(BF16)
        for h in range(heads):
            rows = slice(h * rph, (h + 1) * rph)
            vh = v_ref[:, h * HEAD_DIM:(h + 1) * HEAD_DIM].astype(BF16)
            upd = jnp.dot(w16[rows], vh, preferred_element_type=F32)
            if diff:
                acc_sc[rows, :] = alpha[rows] * acc_sc[rows, :] + upd
            else:
                acc_sc[rows, :] += upd

    @pl.when(s == 0)
    def _():
        step(kn_ref, vn_ref, True)

    @pl.when(s > 0)
    def _():
        step(kc_ref, vc_ref, False)

    @pl.when(s == n_pages)
    def _():
        if diff:
            lam = _diff_lambda(lam_ref, lam_init)
        for h in range(heads):
            rows = slice(h * rph, (h + 1) * rph)
            a = acc_sc[rows, :]
            cols = slice(h * HEAD_DIM, (h + 1) * HEAD_DIM)
            if diff:
                on = a / l_sc[rows, :]
                od = on[0:dec_seq] - lam * on[dec_seq:2 * dec_seq]
                o_ref[:, cols] = _sub_rms(od, gain_ref[...], lam_init)
            else:
                o_ref[:, cols] = a[0:dec_seq]


def _decode_attention(page_table, qblk, k_new, v_new, cache_k, cache_v, lam_vec, gain, *, layer, diff, lam_init,
                      dec_seq):
    nb, n_pages = page_table.shape
    _, _, page, d = cache_k.shape
    heads = d // HEAD_DIM
    rows = heads * DECODE_ROWS_PER_HEAD

    def cache_map(b, s, pt):
        return (layer, pt[b, n_pages - jnp.maximum(s, 1)], 0, 0)

    per_seq = lambda b, s, pt: (b, 0, 0)
    grid_spec = pltpu.PrefetchScalarGridSpec(
        num_scalar_prefetch=1,
        grid=(nb, n_pages + 1),
        in_specs=[
            pl.BlockSpec((None, rows, d), per_seq),
            pl.BlockSpec((None, page, d), per_seq),
            pl.BlockSpec((None, page, d), per_seq),
            pl.BlockSpec((None, None, page, d), cache_map),
            pl.BlockSpec((None, None, page, d), cache_map),
            pl.BlockSpec(lam_vec.shape, lambda b, s, pt: (0, 0)),
            pl.BlockSpec(gain.shape, lambda b, s, pt: (0, 0)),
        ],
        out_specs=pl.BlockSpec((None, dec_seq, d), per_seq),
        scratch_shapes=[pltpu.VMEM((rows, 1), F32), pltpu.VMEM((rows, 1), F32), pltpu.VMEM((rows, HEAD_DIM), F32)],
    )
    return pl.pallas_call(
        functools.partial(_decode_kernel, diff=diff, n_pages=n_pages, dec_seq=dec_seq, lam_init=lam_init,
                          heads=heads),
        grid_spec=grid_spec,
        out_shape=jax.ShapeDtypeStruct((nb, dec_seq, d), F32),
        compiler_params=_params("arbitrary", "arbitrary"),
        name="diff_decode" if diff else "sb_decode",
    )(page_table, qblk, k_new, v_new, cache_k, cache_v, lam_vec, gain)


def _decode_query_rows(q, *, diff, dec_seq):
    nb, _, d = q.shape
    heads = d // HEAD_DIM
    qh = q.reshape(nb, dec_seq, heads, HEAD_DIM).transpose(0, 2, 1, 3)
    if diff:
        lane = jnp.arange(HEAD_DIM) < DIFF_DK
        qh = jnp.concatenate([jnp.where(lane, qh, 0), jnp.where(lane, 0, qh)], axis=2)
    pad = DECODE_ROWS_PER_HEAD - qh.shape[2]
    qh = jnp.pad(qh, ((0, 0), (0, 0), (0, pad), (0, 0)))
    eye = jnp.eye(heads, dtype=q.dtype)
    blk = qh[:, :, :, None, :] * eye[None, :, None, :, None]
    return blk.reshape(nb, heads * DECODE_ROWS_PER_HEAD, d)


def _oproj_ln_kernel(o_ref, w_ref, x_ref, g_ref, b_ref, y32_ref, y16_ref, *, alpha):
    h = jnp.dot(o_ref[...], w_ref[...], preferred_element_type=F32)
    y = _layer_norm(alpha * x_ref[...] + h, g_ref[...], b_ref[...])
    y32_ref[...] = y
    y16_ref[...] = y.astype(BF16)


def _oproj_ln(o16, w16, x32, g, b, *, alpha):
    tp, d = x32.shape
    tm = ROW_TILE
    row = pl.BlockSpec((tm, d), lambda i: (i, 0))
    vec = pl.BlockSpec((1, d), lambda i: (0, 0))
    return pl.pallas_call(
        functools.partial(_oproj_ln_kernel, alpha=alpha),
        grid=(tp // tm,),
        in_specs=[row, pl.BlockSpec((d, d), lambda i: (0, 0)), row, vec, vec],
        out_specs=[row, row],
        out_shape=[jax.ShapeDtypeStruct((tp, d), F32), jax.ShapeDtypeStruct((tp, d), BF16)],
        compiler_params=_params("arbitrary"),
        name="oproj_ln",
    )(o16, w16, x32, g, b)


def _router_kernel(x_ref, wh_ref, wl_ref, b_ref, eidx_ref, gate_ref, rank_ref, cnt_ref, run_sc, *, tm):
    i = pl.program_id(0)

    @pl.when(i == 0)
    def _():
        run_sc[...] = jnp.zeros(run_sc.shape, F32)

    xh, xl = _split_bf16(x_ref[...])
    wh = wh_ref[...]
    logits = (lax.dot_general(wh, xh, _NT, preferred_element_type=F32)
              + lax.dot_general(wh, xl, _NT, preferred_element_type=F32)
              + lax.dot_general(wl_ref[...], xh, _NT, preferred_element_type=F32))
    scores = 1.0 / (1.0 + jnp.exp(-logits))
    biased = scores + b_ref[...]

    b3 = biased.reshape(N_GROUPS, GROUP_SIZE, tm)
    eio = lax.broadcasted_iota(jnp.int32, b3.shape, 1)
    m1 = jnp.max(b3, axis=1, keepdims=True)
    i1 = jnp.min(jnp.where(b3 == m1, eio, GROUP_SIZE), axis=1, keepdims=True)
    m2 = jnp.max(jnp.where(eio == i1, -jnp.inf, b3), axis=1, keepdims=True)
    grp = (m1 + m2).reshape(N_GROUPS, tm)

    gio = lax.broadcasted_iota(jnp.int32, grp.shape, 0)
    gsel = jnp.zeros(grp.shape, jnp.bool_)
    for _ in range(TOPK_GROUPS):
        mx = jnp.max(grp, axis=0, keepdims=True)
        ix = jnp.min(jnp.where(grp == mx, gio, N_GROUPS), axis=0, keepdims=True)
        hit = gio == ix
        gsel = gsel | hit
        grp = jnp.where(hit, -jnp.inf, grp)
    emask = jnp.broadcast_to(gsel.reshape(N_GROUPS, 1, tm), b3.shape).reshape(N_EXPERTS, tm)

    masked = jnp.where(emask, biased, -jnp.inf)
    eio64 = lax.broadcasted_iota(jnp.int32, masked.shape, 0)
    hits, sels, idxs = [], [], []
    for _ in range(TOP_K):
        mx = jnp.max(masked, axis=0, keepdims=True)
        ix = jnp.min(jnp.where(masked == mx, eio64, N_EXPERTS), axis=0, keepdims=True)
        hit = eio64 == ix
        hits.append(hit)
        idxs.append(ix)
        sels.append(jnp.sum(jnp.where(hit, scores, 0.0), axis=0, keepdims=True))
        masked = jnp.where(hit, -jnp.inf, masked)
    sel = jnp.concatenate(sels, axis=0)
    gate_ref[...] = sel / jnp.sum(sel, axis=0, keepdims=True) * ROUTED_SCALE
    eidx_ref[...] = jnp.concatenate(idxs, axis=0)

    chosen = jnp.zeros(masked.shape, F32)
    for hit in hits:
        chosen = chosen + jnp.where(hit, 1.0, 0.0)
    r = lax.broadcasted_iota(jnp.int32, (tm, tm), 0)
    c = lax.broadcasted_iota(jnp.int32, (tm, tm), 1)
    earlier = jnp.where(r < c, 1.0, 0.0).astype(BF16)
    base = jnp.dot(chosen.astype(BF16), earlier, preferred_element_type=F32) + run_sc[...]
    ranks = [jnp.sum(jnp.where(hit, base, 0.0), axis=0, keepdims=True) for hit in hits]
    rank_ref[...] = jnp.concatenate(ranks, axis=0).astype(jnp.int32)
    run_sc[...] += jnp.sum(chosen, axis=1, keepdims=True)
    cnt_ref[...] = run_sc[...]


def _router(x32, wh, wl, bias):
    tp, d = x32.shape
    tm = ROW_TILE
    tok = pl.BlockSpec((TOP_K, tm), lambda i: (0, i))
    return pl.pallas_call(
        functools.partial(_router_kernel, tm=tm),
        grid=(tp // tm,),
        in_specs=[pl.BlockSpec((tm, d), lambda i: (i, 0)),
                  pl.BlockSpec((N_EXPERTS, d), lambda i: (0, 0)),
                  pl.BlockSpec((N_EXPERTS, d), lambda i: (0, 0)),
                  pl.BlockSpec((N_EXPERTS, 1), lambda i: (0, 0))],
        out_specs=[tok, tok, tok, pl.BlockSpec((N_EXPERTS, 1), lambda i: (0, 0))],
        out_shape=[jax.ShapeDtypeStruct((TOP_K, tp), jnp.int32), jax.ShapeDtypeStruct((TOP_K, tp), F32),
                   jax.ShapeDtypeStruct((TOP_K, tp), jnp.int32), jax.ShapeDtypeStruct((N_EXPERTS, 1), F32)],
        scratch_shapes=[pltpu.VMEM((N_EXPERTS, 1), F32)],
        compiler_params=_params("arbitrary"),
        name="router",
    )(x32, wh, wl, bias)


def _dispatch_kernel(dest_ref, x_ref, xs_in_ref, xs_ref, sem, *, tm):
    del xs_in_ref
    base = pl.program_id(0) * tm

    def body(t, c):
        for k in range(TOP_K):
            slot = dest_ref[(base + t) * TOP_K + k]
            pltpu.make_async_copy(x_ref.at[pl.ds(t, 1), :], xs_ref.at[pl.ds(slot, 1), :], sem).start()
        return c

    lax.fori_loop(0, tm, body, 0)
    for k in range(TOP_K):
        pltpu.make_async_copy(x_ref, xs_ref.at[pl.ds(0, tm), :], sem).wait()


def _dispatch(dest, x32, n_slots):
    tp, d = x32.shape
    tm = ROW_TILE
    grid_spec = pltpu.PrefetchScalarGridSpec(
        num_scalar_prefetch=1,
        grid=(tp // tm,),
        in_specs=[pl.BlockSpec((tm, d), lambda i, dest: (i, 0)), pl.BlockSpec(memory_space=pl.ANY)],
        out_specs=pl.BlockSpec(memory_space=pl.ANY),
        scratch_shapes=[pltpu.SemaphoreType.DMA(())],
    )
    return pl.pallas_call(
        functools.partial(_dispatch_kernel, tm=tm),
        grid_spec=grid_spec,
        out_shape=jax.ShapeDtypeStruct((n_slots, d), F32),
        input_output_aliases={2: 0},
        compiler_params=_params("arbitrary"),
        name="dispatch",
    )(dest, x32, jnp.zeros((n_slots, d), F32))


def _expert_kernel(be_ref, nused_ref, xs_ref, wg_ref, wu_ref, wd_ref, ys_ref, wg16, wu16, wd16):
    i = pl.program_id(0)

    @pl.when(i < nused_ref[0])
    def _():
        prev = be_ref[jnp.maximum(i - 1, 0)]

        @pl.when((i == 0) | (be_ref[i] != prev))
        def _():
            wg16[...] = wg_ref[...].astype(BF16)
            wu16[...] = wu_ref[...].astype(BF16)
            wd16[...] = wd_ref[...].astype(BF16)

        x = xs_ref[...].astype(BF16)
        g = jnp.dot(x, wg16[...], preferred_element_type=F32)
        u = jnp.dot(x, wu16[...], preferred_element_type=F32)
        h = g * (1.0 / (1.0 + jnp.exp(-g))) * u
        ys_ref[...] = jnp.dot(h.astype(BF16), wd16[...], preferred_element_type=F32)

    @pl.when(i >= nused_ref[0])
    def _():
        ys_ref[...] = jnp.zeros(ys_ref.shape, F32)


def _experts(block_expert, n_used, xs, wg, wu, wd):
    n_slots, d = xs.shape
    f = wg.shape[-1]
    tb = EXPERT_ROWS

    def rows(i, be, nu):
        return (jnp.minimum(i, nu[0] - 1), 0)

    def weights(i, be, nu):
        return (be[jnp.minimum(i, nu[0] - 1)], 0, 0)

    grid_spec = pltpu.PrefetchScalarGridSpec(
        num_scalar_prefetch=2,
        grid=(n_slots // tb,),
        in_specs=[pl.BlockSpec((tb, d), rows),
                  pl.BlockSpec((None, d, f), weights),
                  pl.BlockSpec((None, d, f), weights),
                  pl.BlockSpec((None, f, d), weights)],
        out_specs=pl.BlockSpec((tb, d), lambda i, be, nu: (i, 0)),
        scratch_shapes=[pltpu.VMEM((d, f), BF16), pltpu.VMEM((d, f), BF16), pltpu.VMEM((f, d), BF16)],
    )
    return pl.pallas_call(
        _expert_kernel,
        grid_spec=grid_spec,
        out_shape=jax.ShapeDtypeStruct((n_slots, d), F32),
        compiler_params=_params("arbitrary"),
        name="experts",
    )(block_expert, n_used, xs, wg, wu, wd)


def _combine_kernel(dest_ref, x32_ref, x16_ref, gate_ref, sg_ref, su_ref, sd_ref, g_ref, b_ref, ys_ref,
                    y32_ref, y16_ref, buf, sem, *, tm, alpha):
    base = pl.program_id(0) * tm

    def body(t, c):
        for k in range(TOP_K):
            slot = dest_ref[(base + t) * TOP_K + k]
            pltpu.make_async_copy(ys_ref.at[pl.ds(slot, 1), :], buf.at[k, pl.ds(t, 1), :], sem).start()
        return c

    lax.fori_loop(0, tm, body, 0)

    x16 = x16_ref[...]
    g = jnp.dot(x16, sg_ref[...], preferred_element_type=F32)
    u = jnp.dot(x16, su_ref[...], preferred_element_type=F32)
    h = g * (1.0 / (1.0 + jnp.exp(-g))) * u
    y = jnp.dot(h.astype(BF16), sd_ref[...], preferred_element_type=F32)

    for k in range(TOP_K):
        pltpu.make_async_copy(ys_ref.at[pl.ds(0, tm), :], buf.at[k], sem).wait()
    gate = gate_ref[...]
    for k in range(TOP_K):
        y = y + buf[k] * gate[:, k:k + 1]
    out = _layer_norm(alpha * x32_ref[...] + y, g_ref[...], b_ref[...])
    y32_ref[...] = out
    y16_ref[...] = out.astype(BF16)


def _combine(dest, x32, x16, gate, sg, su, sd, g, b, ys, *, alpha):
    tp, d = x32.shape
    f = sg.shape[-1]
    tm = ROW_TILE
    row = pl.BlockSpec((tm, d), lambda i, dest: (i, 0))
    vec = pl.BlockSpec((1, d), lambda i, dest: (0, 0))
    grid_spec = pltpu.PrefetchScalarGridSpec(
        num_scalar_prefetch=1,
        grid=(tp // tm,),
        in_specs=[row, row,
                  pl.BlockSpec((tm, TOP_K), lambda i, dest: (i, 0)),
                  pl.BlockSpec((d, f), lambda i, dest: (0, 0)),
                  pl.BlockSpec((d, f), lambda i, dest: (0, 0)),
                  pl.BlockSpec((f, d), lambda i, dest: (0, 0)),
                  vec, vec,
                  pl.BlockSpec(memory_space=pl.ANY)],
        out_specs=[row, row],
        scratch_shapes=[pltpu.VMEM((TOP_K, tm, d), F32), pltpu.SemaphoreType.DMA(())],
    )
    return pl.pallas_call(
        functools.partial(_combine_kernel, tm=tm, alpha=alpha),
        grid_spec=grid_spec,
        out_shape=[jax.ShapeDtypeStruct((tp, d), F32), jax.ShapeDtypeStruct((tp, d), BF16)],
        compiler_params=_params("arbitrary"),
        name="combine",
    )(dest, x32, x16, gate, sg, su, sd, g, b, ys)


def _moe(x32, x16, *, w_r, b_r, wg, wu, wd, sg, su, sd, g, b, alpha):
    tp, d = x32.shape
    tb = EXPERT_ROWS
    wrt = w_r.T
    wh, wl = _split_bf16(wrt)
    eidx, gate, rank, cnt = _router(x32, wh, wl, b_r.reshape(N_EXPERTS, 1))

    counts = cnt[:, 0].astype(jnp.int32)
    blocks = (counts + tb - 1) // tb
    blk_end = jnp.cumsum(blocks)
    blk_start = blk_end - blocks
    n_blocks = (tp * TOP_K) // tb + N_EXPERTS
    dest = (blk_start[eidx] * tb + rank).T.reshape(tp * TOP_K)
    block_expert = jnp.minimum(
        jnp.searchsorted(blk_end, jnp.arange(n_blocks, dtype=blk_end.dtype), side='right'), N_EXPERTS - 1
    ).astype(jnp.int32)
    n_used = blk_end[-1:].astype(jnp.int32)

    xs = _dispatch(dest, x32, n_blocks * tb)
    ys = _experts(block_expert, n_used, xs, wg, wu, wd)
    return _combine(dest, x32, x16, gate.T, sg.astype(BF16), su.astype(BF16), sd.astype(BF16), g, b, ys,
                    alpha=alpha)


def _rope_tables(pos):
    half = DIFF_DK // 2
    inv = ROPE_THETA ** (-jnp.arange(half, dtype=F32) / half)
    ang = pos.astype(F32)[:, None] * inv[None, :]
    cos = jnp.concatenate([jnp.cos(ang)] * 4, axis=-1)
    sin = jnp.concatenate([-jnp.sin(ang), jnp.sin(ang)] * 2, axis=-1)
    return cos, sin


def kernel(x_prompt, x_sample, cache_k, cache_v, page_table, meta_tokens, attn_w_qkv, attn_w_o, diff_lambda, diff_subln, ln_mix_g, ln_mix_b, router_w, router_bias, expert_w_gate, expert_w_up, expert_w_down, shared_w_gate, shared_w_up, shared_w_down, ln_ffn_g, ln_ffn_b):
    bp, seq, d = x_prompt.shape
    nb, dec_seq, _ = x_sample.shape
    depth, _, page, _ = cache_k.shape
    assert bp == 1 and d % HEAD_DIM == 0 and 2 * dec_seq <= DECODE_ROWS_PER_HEAD
    n_prompt = N_META + seq
    n_dec = nb * dec_seq
    n_tok = n_prompt + n_dec
    tp = -(-n_tok // ROW_TILE) * ROW_TILE
    past_len = page_table.shape[1] * page
    alpha = (2 * depth) ** 0.25

    x32 = jnp.concatenate([meta_tokens.astype(F32), x_prompt[0], x_sample.reshape(n_dec, d),
                           jnp.zeros((tp - n_tok, d), F32)], axis=0)
    x16 = x32.astype(BF16)
    pos = jnp.concatenate([jnp.arange(n_prompt, dtype=jnp.int32),
                           jnp.tile(past_len + jnp.arange(dec_seq, dtype=jnp.int32), nb),
                           jnp.zeros((tp - n_tok,), jnp.int32)])
    cos, sin = _rope_tables(pos)
    dec = slice(n_prompt, n_tok)

    kp_rows, vp_rows, ks_rows, vs_rows = [], [], [], []
    for i in range(depth):
        diff = i % 2 == 0
        j = i // 2
        lam_init = 0.8 - 0.6 * math.exp(-0.3 * i)
        w3 = attn_w_qkv[i].reshape(d, 3, d).transpose(1, 0, 2).astype(BF16)
        q_scale = (DIFF_DK if diff else HEAD_DIM) ** -0.5
        qkv32, qkv16 = _qkv_proj(x16, w3, cos, sin, rope=diff, q_scale=q_scale)
        lam_vec = diff_lambda[j].astype(F32)
        gain = diff_subln[j].astype(F32).reshape(1, HEAD_DIM)

        o16 = _prompt_attention(qkv16, lam_vec, gain, diff=diff, lam_init=lam_init)

        qblk = _decode_query_rows(qkv16[0, dec].reshape(nb, dec_seq, d), diff=diff, dec_seq=dec_seq)
        k_new = qkv32[1, dec].reshape(nb, dec_seq, d)
        v_new = qkv32[2, dec].reshape(nb, dec_seq, d)
        pad_new = ((0, 0), (0, page - dec_seq), (0, 0))
        o_dec = _decode_attention(page_table, qblk, jnp.pad(k_new, pad_new), jnp.pad(v_new, pad_new),
                                  cache_k, cache_v, lam_vec, gain, layer=i, diff=diff, lam_init=lam_init,
                                  dec_seq=dec_seq)
        o16 = lax.dynamic_update_slice(o16, o_dec.reshape(n_dec, d).astype(BF16), (n_prompt, 0))

        kp_rows.append(qkv32[1, :n_prompt][None])
        vp_rows.append(qkv32[2, :n_prompt][None])
        ks_rows.append(k_new)
        vs_rows.append(v_new)

        x32, x16 = _oproj_ln(o16, attn_w_o[i].astype(BF16), x32, ln_mix_g[i].reshape(1, d),
                             ln_mix_b[i].reshape(1, d), alpha=alpha)
        x32, x16 = _moe(x32, x16, w_r=router_w[i], b_r=router_bias[i], wg=expert_w_gate[i], wu=expert_w_up[i],
                        wd=expert_w_down[i], sg=shared_w_gate[i], su=shared_w_up[i], sd=shared_w_down[i],
                        g=ln_ffn_g[i].reshape(1, d), b=ln_ffn_b[i].reshape(1, d), alpha=alpha)

    y_prompt = x32[N_META:n_prompt][None]
    y_sample = x32[dec].reshape(nb, dec_seq, d)
    return (y_prompt, y_sample, jnp.stack(kp_rows), jnp.stack(vp_rows), jnp.stack(ks_rows), jnp.stack(vs_rows))
```

```python
import functools
import math

import jax
import jax.numpy as jnp
from jax import lax
from jax.experimental import pallas as pl
from jax.experimental.pallas import tpu as pltpu

F32 = jnp.float32
BF16 = jnp.bfloat16

N_META = 16
HEAD_DIM = 128
DIFF_DK = 64
ROPE_THETA = 10000.0
N_EXPERTS = 64
N_GROUPS = 8
GROUP_SIZE = N_EXPERTS // N_GROUPS
TOPK_GROUPS = 4
TOP_K = 8
ROUTED_SCALE = 2.5
LN_EPS = 1e-5
NEG_INF = -1e30
LOG2_E = 1.4426950408889634
SB_LOG_FLOOR = -104.0

LANES = 128
ROW_TILE = 256
KV_TILE = 512
EXPERT_ROWS = 256
DECODE_COLS_PER_HEAD = 8
VMEM_LIMIT_BYTES = 56 * 1024 * 1024

_NT = (((1,), (1,)), ((), ()))


def _params(*sem):
    return pltpu.CompilerParams(dimension_semantics=sem, vmem_limit_bytes=VMEM_LIMIT_BYTES)


def _softplus(z):
    return jnp.maximum(z, 0.0) + jnp.log(1.0 + jnp.exp(-jnp.abs(z)))


def _layer_norm(y, g, b):
    mu = jnp.mean(y, axis=1, keepdims=True)
    d = y - mu
    var = jnp.mean(d * d, axis=1, keepdims=True)
    return d * lax.rsqrt(var + LN_EPS) * g + b


def _split_bf16(x):
    hi = x.astype(BF16)
    lo = (x - hi.astype(F32)).astype(BF16)
    return hi, lo


def _later_keys(n):
    r = lax.broadcasted_iota(jnp.int32, (n, n), 0)
    c = lax.broadcasted_iota(jnp.int32, (n, n), 1)
    return jnp.where(c > r, 1.0, 0.0).astype(BF16)


def _suffix_sum(later, lk):
    hi, lo = _split_bf16(lk)
    return (jnp.dot(later, hi, preferred_element_type=F32)
            + jnp.dot(later, lo, preferred_element_type=F32))


def _diff_lambda(lam_ref, lam_init):
    lv = lam_ref[...]
    a = jnp.sum(lv[0:1] * lv[1:2], axis=(0, 1), keepdims=True)
    b = jnp.sum(lv[2:3] * lv[3:4], axis=(0, 1), keepdims=True)
    return jnp.exp(a) - jnp.exp(b) + lam_init


def _rope(val, cos, sin):
    lane = lax.broadcasted_iota(jnp.int32, cos.shape, 1)
    first_half = (lane % DIFF_DK) < (DIFF_DK // 2)
    outs = []
    for s in range(val.shape[1] // LANES):
        xs = val[:, s * LANES:(s + 1) * LANES]
        rot = jnp.where(first_half, pltpu.roll(xs, LANES - DIFF_DK // 2, 1), pltpu.roll(xs, DIFF_DK // 2, 1))
        outs.append(xs * cos + rot * sin)
    return jnp.concatenate(outs, axis=1)


def _q_proj_kernel(x_ref, w_ref, cos_ref, sin_ref, q16_ref, *, rope, scale):
    val = jnp.dot(x_ref[...], w_ref[...], preferred_element_type=F32)
    if rope:
        val = _rope(val, cos_ref[...], sin_ref[...])
    q16_ref[...] = (val * scale).astype(BF16)


def _kv_proj_kernel(x_ref, w_ref, cos_ref, sin_ref, all_in_ref, o16_ref, prompt_ref, dec_ref, stage, sem, *,
                    rope, layer, n_prompt, dec_block, dec_off, n_dec):
    del all_in_ref
    i = pl.program_id(0)
    tm = stage.shape[0]
    n_full, rem = divmod(n_prompt, tm)
    last = pl.num_programs(0) - 1
    val = jnp.dot(x_ref[...], w_ref[...], preferred_element_type=F32)
    if rope:
        val = _rope(val, cos_ref[...], sin_ref[...])
    o16_ref[...] = val.astype(BF16)

    def full_copy(blk):
        return pltpu.make_async_copy(stage, prompt_ref.at[layer, 0, pl.ds(blk * tm, tm)], sem)

    def rem_copy():
        return pltpu.make_async_copy(stage.at[pl.ds(0, rem)], prompt_ref.at[layer, 0, pl.ds(n_full * tm, rem)], sem)

    @pl.when((i >= 1) & (i <= n_full))
    def _():
        full_copy(i - 1).wait()

    if rem:
        @pl.when(i == n_full + 1)
        def _():
            rem_copy().wait()

    @pl.when(i < n_full)
    def _():
        stage[...] = val
        full_copy(i).start()

        @pl.when(i == last)
        def _():
            full_copy(i).wait()

    if rem:
        @pl.when(i == n_full)
        def _():
            stage[...] = val
            rem_copy().start()

            @pl.when(i == last)
            def _():
                rem_copy().wait()

    @pl.when(i == dec_block)
    def _():
        dec_ref[...] = val[dec_off:dec_off + n_dec]


def _q_proj(x16, w16, cos, sin, *, rope, scale):
    tp, d = x16.shape
    tm = ROW_TILE
    row = pl.BlockSpec((tm, d), lambda i: (i, 0))
    tab = pl.BlockSpec((tm, LANES), lambda i: (i, 0))
    return pl.pallas_call(
        functools.partial(_q_proj_kernel, rope=rope, scale=scale),
        grid=(tp // tm,),
        in_specs=[row, pl.BlockSpec((d, d), lambda i: (0, 0)), tab, tab],
        out_specs=row,
        out_shape=jax.ShapeDtypeStruct((tp, d), BF16),
        compiler_params=_params("arbitrary"),
        name="q_proj",
    )(x16, w16, cos, sin)


def _kv_proj(x16, w16, cos, sin, all_rows, *, rope, layer, n_prompt, n_dec):
    tp, d = x16.shape
    tm = ROW_TILE
    dec_block, dec_off = divmod(n_prompt, tm)
    assert dec_off + n_dec <= tm and dec_off % 8 == 0, "decode rows must sit inside one row tile"
    row = pl.BlockSpec((tm, d), lambda i: (i, 0))
    tab = pl.BlockSpec((tm, LANES), lambda i: (i, 0))
    return pl.pallas_call(
        functools.partial(_kv_proj_kernel, rope=rope, layer=layer, n_prompt=n_prompt, dec_block=dec_block,
                          dec_off=dec_off, n_dec=n_dec),
        grid=(tp // tm,),
        in_specs=[row, pl.BlockSpec((d, d), lambda i: (0, 0)), tab, tab, pl.BlockSpec(memory_space=pl.ANY)],
        out_specs=[row, pl.BlockSpec(memory_space=pl.ANY), pl.BlockSpec((n_dec, d), lambda i: (0, 0))],
        out_shape=[jax.ShapeDtypeStruct((tp, d), BF16),
                   jax.ShapeDtypeStruct(all_rows.shape, F32),
                   jax.ShapeDtypeStruct((n_dec, d), F32)],
        scratch_shapes=[pltpu.VMEM((tm, d), F32), pltpu.SemaphoreType.DMA(())],
        input_output_aliases={4: 1},
        compiler_params=_params("arbitrary"),
        name="kv_proj",
    )(x16, w16, cos, sin, all_rows)


def _diff_attn_kernel(q_ref, k_ref, vt_ref, lam_ref, gain_ref, o_ref, m_sc, l_sc, acc_sc, *, tq, tk, lam_init):
    qi = pl.program_id(1)
    q = q_ref[...]
    lane = lax.broadcasted_iota(jnp.int32, q.shape, 1)
    zero = jnp.zeros_like(q)
    q2 = jnp.concatenate([jnp.where(lane < DIFF_DK, q, zero), jnp.where(lane >= DIFF_DK, q, zero)], axis=0)
    m_sc[...] = jnp.full(m_sc.shape, NEG_INF, F32)
    l_sc[...] = jnp.zeros(l_sc.shape, F32)
    acc_sc[...] = jnp.zeros(acc_sc.shape, F32)

    def block(j, masked):
        start = pl.multiple_of(j * tk, tk)
        kb = k_ref[pl.ds(start, tk), :]
        vtb = vt_ref[:, pl.ds(start, tk)]
        s = lax.dot_general(kb, q2, _NT, preferred_element_type=F32)
        if masked:
            key = start + lax.broadcasted_iota(jnp.int32, s.shape, 0)
            col = lax.broadcasted_iota(jnp.int32, s.shape, 1)
            qpos = qi * tq + jnp.where(col >= tq, col - tq, col)
            s = jnp.where(key <= qpos, s, NEG_INF)
        m_prev = m_sc[...]
        m_new = jnp.maximum(m_prev, jnp.max(s, axis=0, keepdims=True))
        alpha = jnp.exp2(m_prev - m_new)
        p = jnp.exp2(s - m_new)
        l_sc[...] = alpha * l_sc[...] + jnp.sum(p, axis=0, keepdims=True)
        acc_sc[...] = alpha * acc_sc[...] + jnp.dot(vtb, p.astype(BF16), preferred_element_type=F32)
        m_sc[...] = m_new

    last = (qi * tq) // tk

    def body(j, c):
        block(j, False)
        return c

    lax.fori_loop(0, last, body, 0)
    block(last, True)

    o = acc_sc[...] / l_sc[...]
    lam = _diff_lambda(lam_ref, lam_init)
    od = o[:, :tq] - lam * o[:, tq:]
    ms = jnp.mean(od * od, axis=0, keepdims=True)
    on = od * lax.rsqrt(ms + LN_EPS) * gain_ref[...] * (1.0 - lam_init)
    o_ref[...] = on.T.astype(BF16)


def _sb_attn_kernel(q_ref, k_ref, vt_ref, o_ref, r_sc, acc_sc, *, tq, tk):
    qi = pl.program_id(1)
    q = q_ref[...]
    later = _later_keys(tk)
    r_sc[...] = jnp.zeros(r_sc.shape, F32)
    acc_sc[...] = jnp.zeros(acc_sc.shape, F32)

    def block(j, masked):
        start = pl.multiple_of(j * tk, tk)
        kb = k_ref[pl.ds(start, tk), :]
        vtb = vt_ref[:, pl.ds(start, tk)]
        z = lax.dot_general(kb, q, _NT, preferred_element_type=F32)
        sp = _softplus(z)
        lk = -sp
        if masked:
            key = start + lax.broadcasted_iota(jnp.int32, z.shape, 0)
            qpos = qi * tq + lax.broadcasted_iota(jnp.int32, z.shape, 1)
            valid = key < qpos
            lk = jnp.where(valid, lk, 0.0)
        r_prev = r_sc[...]
        w = jnp.exp(z - sp + _suffix_sum(later, lk) + r_prev)
        if masked:
            w = jnp.where(valid, w, 0.0)
        acc_sc[...] += jnp.dot(vtb, w.astype(BF16), preferred_element_type=F32)
        r_sc[...] = r_prev + jnp.sum(lk, axis=0, keepdims=True)

    last = (qi * tq) // tk
    block(last, True)

    def cond(j):
        return (j >= 0) & (jnp.max(r_sc[...]) > SB_LOG_FLOOR)

    def body(j):
        block(j, False)
        return j - 1

    lax.while_loop(cond, body, last - 1)
    o_ref[...] = acc_sc[...].T.astype(BF16)


def _prompt_attention(q16, k16, vt16, lam_vec, gain_col, *, diff, lam_init):
    tp, d = q16.shape
    tq, tk = ROW_TILE, KV_TILE
    heads = d // HEAD_DIM
    qspec = pl.BlockSpec((tq, HEAD_DIM), lambda h, i: (i, h))
    kspec = pl.BlockSpec((tp, HEAD_DIM), lambda h, i: (0, h))
    vspec = pl.BlockSpec((HEAD_DIM, tp), lambda h, i: (h, 0))
    ospec = pl.BlockSpec((tq, HEAD_DIM), lambda h, i: (i, h))
    out_shape = jax.ShapeDtypeStruct((tp, d), BF16)
    if diff:
        return pl.pallas_call(
            functools.partial(_diff_attn_kernel, tq=tq, tk=tk, lam_init=lam_init),
            grid=(heads, tp // tq),
            in_specs=[qspec, kspec, vspec,
                      pl.BlockSpec(lam_vec.shape, lambda h, i: (0, 0)),
                      pl.BlockSpec(gain_col.shape, lambda h, i: (0, 0))],
            out_specs=ospec,
            out_shape=out_shape,
            scratch_shapes=[pltpu.VMEM((1, 2 * tq), F32), pltpu.VMEM((1, 2 * tq), F32),
                            pltpu.VMEM((HEAD_DIM, 2 * tq), F32)],
            compiler_params=_params("arbitrary", "arbitrary"),
            name="diff_attention",
        )(q16, k16, vt16, lam_vec, gain_col)
    return pl.pallas_call(
        functools.partial(_sb_attn_kernel, tq=tq, tk=tq),
        grid=(heads, tp // tq),
        in_specs=[qspec, kspec, vspec],
        out_specs=ospec,
        out_shape=out_shape,
        scratch_shapes=[pltpu.VMEM((1, tq), F32), pltpu.VMEM((HEAD_DIM, tq), F32)],
        compiler_params=_params("arbitrary", "arbitrary"),
        name="sb_attention",
    )(q16, k16, vt16)


def _per_row(stat):
    return jnp.broadcast_to(stat, (HEAD_DIM, stat.shape[1])).T


def _decode_pv(acc_sc, ws, v_refs, heads, alpha=None):
    cph = DECODE_COLS_PER_HEAD
    w16s = [w.astype(BF16) for w in ws]
    first = lax.broadcasted_iota(jnp.int32, (2 * cph, HEAD_DIM), 0) < cph
    for g in range(heads // 2):
        rows = slice(2 * g * cph, (2 * g + 2) * cph)
        upd = None
        for w16, v_ref in zip(w16s, v_refs):
            wn = w16[rows]
            lo = jnp.dot(wn, v_ref[:, (2 * g) * HEAD_DIM:(2 * g + 1) * HEAD_DIM].astype(BF16),
                         preferred_element_type=F32)
            hi = jnp.dot(wn, v_ref[:, (2 * g + 1) * HEAD_DIM:(2 * g + 2) * HEAD_DIM].astype(BF16),
                         preferred_element_type=F32)
            part = jnp.where(first, lo, hi)
            upd = part if upd is None else upd + part
        if alpha is None:
            acc_sc[rows, :] += upd
        else:
            acc_sc[rows, :] = alpha[rows] * acc_sc[rows, :] + upd


def _diff_decode_kernel(pt_ref, qb_ref, kn_ref, vn_ref, ka_ref, va_ref, kb_ref, vb_ref, lam_ref, gain_ref, o_ref,
                        m_sc, l_sc, acc_sc, *, n_steps, dec_seq, lam_init, heads):
    s = pl.program_id(1)
    cph = DECODE_COLS_PER_HEAD

    @pl.when(s == 0)
    def _():
        m_sc[...] = jnp.full(m_sc.shape, NEG_INF, F32)
        l_sc[...] = jnp.zeros(l_sc.shape, F32)
        acc_sc[...] = jnp.zeros(acc_sc.shape, F32)

    def step(k_refs, v_refs, is_new):
        qb = qb_ref[...]
        scs = [jnp.dot(k_ref[...].astype(BF16), qb, preferred_element_type=F32) for k_ref in k_refs]
        if is_new:
            key = lax.broadcasted_iota(jnp.int32, scs[0].shape, 0)
            col = lax.broadcasted_iota(jnp.int32, scs[0].shape, 1)
            scs = [jnp.where((key <= (col % cph) % dec_seq) & (key < dec_seq), scs[0], NEG_INF)]
        m_prev = m_sc[...]
        m_new = m_prev
        for sc in scs:
            m_new = jnp.maximum(m_new, jnp.max(sc, axis=0, keepdims=True))
        alpha = jnp.exp2(m_prev - m_new)
        ws = [jnp.exp2(sc - m_new) for sc in scs]
        l_new = alpha * l_sc[...]
        for w in ws:
            l_new = l_new + jnp.sum(w, axis=0, keepdims=True)
        l_sc[...] = l_new
        m_sc[...] = m_new
        _decode_pv(acc_sc, [w.T for w in ws], v_refs, heads, _per_row(alpha))

    @pl.when(s == 0)
    def _():
        step([kn_ref], [vn_ref], True)

    @pl.when(s > 0)
    def _():
        step([ka_ref, kb_ref], [va_ref, vb_ref], False)

    @pl.when(s == n_steps - 1)
    def _():
        lam = _diff_lambda(lam_ref, lam_init)
        l_rows = _per_row(l_sc[...])
        for h in range(heads):
            rows = slice(h * cph, (h + 1) * cph)
            on = acc_sc[rows, :] / l_rows[rows]
            od = on[0:dec_seq] - lam * on[dec_seq:2 * dec_seq]
            ms = jnp.mean(od * od, axis=1, keepdims=True)
            o_ref[:, h * HEAD_DIM:(h + 1) * HEAD_DIM] = (od * lax.rsqrt(ms + LN_EPS) * gain_ref[...]
                                                         * (1.0 - lam_init))


def _diff_decode(page_table, qblk, k_new, v_new, cache_k, cache_v, lam_vec, gain_row, *, layer, lam_init, dec_seq):
    nb, n_pages = page_table.shape
    _, _, page, d = cache_k.shape
    heads = d // HEAD_DIM
    cols = heads * DECODE_COLS_PER_HEAD
    assert n_pages % 2 == 0
    n_steps = n_pages // 2 + 1

    def page_map(which):
        return lambda b, s, pt: (layer, pt[b, 2 * (jnp.maximum(s, 1) - 1) + which], 0, 0)

    per_seq = lambda b, s, pt: (b, 0, 0)
    cache_spec = lambda which: pl.BlockSpec((None, None, page, d), page_map(which))
    grid_spec = pltpu.PrefetchScalarGridSpec(
        num_scalar_prefetch=1,
        grid=(nb, n_steps),
        in_specs=[
            pl.BlockSpec((None, d, cols), per_seq),
            pl.BlockSpec((None, page, d), per_seq),
            pl.BlockSpec((None, page, d), per_seq),
            cache_spec(0), cache_spec(0), cache_spec(1), cache_spec(1),
            pl.BlockSpec(lam_vec.shape, lambda b, s, pt: (0, 0)),
            pl.BlockSpec(gain_row.shape, lambda b, s, pt: (0, 0)),
        ],
        out_specs=pl.BlockSpec((None, dec_seq, d), per_seq),
        scratch_shapes=[pltpu.VMEM((1, cols), F32), pltpu.VMEM((1, cols), F32), pltpu.VMEM((cols, HEAD_DIM), F32)],
    )
    return pl.pallas_call(
        functools.partial(_diff_decode_kernel, n_steps=n_steps, dec_seq=dec_seq, lam_init=lam_init, heads=heads),
        grid_spec=grid_spec,
        out_shape=jax.ShapeDtypeStruct((nb, dec_seq, d), F32),
        compiler_params=_params("arbitrary", "arbitrary"),
        name="diff_decode",
    )(page_table, qblk, k_new, v_new, cache_k, cache_v, cache_k, cache_v, lam_vec, gain_row)


def _sb_decode_kernel(pt_ref, qb_ref, kn_ref, vn_ref, kc_hbm, vc_hbm, o_ref, kbuf, vbuf, sems, r_sc, acc_sc, *,
                      layer, n_pages, dec_seq, heads):
    b = pl.program_id(0)
    cph = DECODE_COLS_PER_HEAD
    page = kbuf.shape[1]
    later = _later_keys(page)

    def copies(p, slot):
        pg = pt_ref[b, p]
        return (pltpu.make_async_copy(kc_hbm.at[layer, pg], kbuf.at[slot], sems.at[0, slot]),
                pltpu.make_async_copy(vc_hbm.at[layer, pg], vbuf.at[slot], sems.at[1, slot]))

    def fetch(p, slot):
        for c in copies(p, slot):
            c.start()

    def wait(p, slot):
        for c in copies(p, slot):
            c.wait()

    fetch(n_pages - 1, 0)
    r_sc[...] = jnp.zeros(r_sc.shape, F32)
    acc_sc[...] = jnp.zeros(acc_sc.shape, F32)

    def step(k_ref, v_ref, is_new):
        z = jnp.dot(k_ref[...].astype(BF16), qb_ref[...], preferred_element_type=F32)
        sp = _softplus(z)
        lk = -sp
        if is_new:
            key = lax.broadcasted_iota(jnp.int32, z.shape, 0)
            col = lax.broadcasted_iota(jnp.int32, z.shape, 1)
            valid = key < col % cph
            lk = jnp.where(valid, lk, 0.0)
        r_prev = r_sc[...]
        w = jnp.exp(z - sp + _suffix_sum(later, lk) + r_prev)
        if is_new:
            w = jnp.where(valid, w, 0.0)
        r_sc[...] = r_prev + jnp.sum(lk, axis=0, keepdims=True)
        _decode_pv(acc_sc, [w.T], [v_ref], heads)

    step(kn_ref, vn_ref, True)
    real_col = lax.broadcasted_iota(jnp.int32, r_sc.shape, 1) % cph < dec_seq

    def cond(p):
        return (p >= 0) & (jnp.max(jnp.where(real_col, r_sc[...], NEG_INF)) > SB_LOG_FLOOR)

    def body(p):
        slot = (n_pages - 1 - p) % 2
        wait(p, slot)

        @pl.when(p > 0)
        def _():
            fetch(p - 1, 1 - slot)

        step(kbuf.at[slot], vbuf.at[slot], False)
        return p - 1

    p_end = lax.while_loop(cond, body, n_pages - 1)

    @pl.when(p_end >= 0)
    def _():
        wait(p_end, (n_pages - 1 - p_end) % 2)

    for h in range(heads):
        o_ref[:, h * HEAD_DIM:(h + 1) * HEAD_DIM] = acc_sc[h * cph:h * cph + dec_seq, :]


def _sb_decode(page_table, qblk, k_new, v_new, cache_k, cache_v, *, layer, dec_seq):
    nb, n_pages = page_table.shape
    _, _, page, d = cache_k.shape
    heads = d // HEAD_DIM
    cols = heads * DECODE_COLS_PER_HEAD
    per_seq = lambda b, pt: (b, 0, 0)
    grid_spec = pltpu.PrefetchScalarGridSpec(
        num_scalar_prefetch=1,
        grid=(nb,),
        in_specs=[
            pl.BlockSpec((None, d, cols), per_seq),
            pl.BlockSpec((None, page, d), per_seq),
            pl.BlockSpec((None, page, d), per_seq),
            pl.BlockSpec(memory_space=pl.ANY),
            pl.BlockSpec(memory_space=pl.ANY),
        ],
        out_specs=pl.BlockSpec((None, dec_seq, d), per_seq),
        scratch_shapes=[pltpu.VMEM((2, page, d), F32), pltpu.VMEM((2, page, d), F32),
                        pltpu.SemaphoreType.DMA((2, 2)),
                        pltpu.VMEM((1, cols), F32), pltpu.VMEM((cols, HEAD_DIM), F32)],
    )
    return pl.pallas_call(
        functools.partial(_sb_decode_kernel, layer=layer, n_pages=n_pages, dec_seq=dec_seq, heads=heads),
        grid_spec=grid_spec,
        out_shape=jax.ShapeDtypeStruct((nb, dec_seq, d), F32),
        compiler_params=_params("arbitrary"),
        name="sb_decode",
    )(page_table, qblk, k_new, v_new, cache_k, cache_v)


def _decode_query_cols(q, *, diff, dec_seq):
    nb, _, d = q.shape
    heads = d // HEAD_DIM
    qh = q.reshape(nb, dec_seq, heads, HEAD_DIM).transpose(0, 2, 1, 3)
    if diff:
        lane = jnp.arange(HEAD_DIM) < DIFF_DK
        qh = jnp.concatenate([jnp.where(lane, qh, 0), jnp.where(lane, 0, qh)], axis=2)
    qh = jnp.pad(qh, ((0, 0), (0, 0), (0, DECODE_COLS_PER_HEAD - qh.shape[2]), (0, 0)))
    eye = jnp.eye(heads, dtype=q.dtype)
    blk = qh.transpose(0, 1, 3, 2)[:, :, :, None, :] * eye[None, :, None, :, None]
    return blk.reshape(nb, d, heads * DECODE_COLS_PER_HEAD)


def _oproj_ln_kernel(o_ref, w_ref, x_ref, g_ref, b_ref, y32_ref, y16_ref, *, alpha):
    h = jnp.dot(o_ref[...], w_ref[...], preferred_element_type=F32)
    y = _layer_norm(alpha * x_ref[...] + h, g_ref[...], b_ref[...])
    y32_ref[...] = y
    y16_ref[...] = y.astype(BF16)


def _oproj_ln(o16, w16, x32, g, b, *, alpha):
    tp, d = x32.shape
    tm = ROW_TILE
    row = pl.BlockSpec((tm, d), lambda i: (i, 0))
    vec = pl.BlockSpec((1, d), lambda i: (0, 0))
    return pl.pallas_call(
        functools.partial(_oproj_ln_kernel, alpha=alpha),
        grid=(tp // tm,),
        in_specs=[row, pl.BlockSpec((d, d), lambda i: (0, 0)), row, vec, vec],
        out_specs=[row, row],
        out_shape=[jax.ShapeDtypeStruct((tp, d), F32), jax.ShapeDtypeStruct((tp, d), BF16)],
        compiler_params=_params("arbitrary"),
        name="oproj_ln",
    )(o16, w16, x32, g, b)


def _router_kernel(x_ref, wh_ref, wl_ref, b_ref, eidx_ref, gate_ref, rank_ref, cnt_ref, run_sc, *, tm):
    i = pl.program_id(0)

    @pl.when(i == 0)
    def _():
        run_sc[...] = jnp.zeros(run_sc.shape, F32)

    xh, xl = _split_bf16(x_ref[...])
    wh = wh_ref[...]
    logits = (lax.dot_general(wh, xh, _NT, preferred_element_type=F32)
              + lax.dot_general(wh, xl, _NT, preferred_element_type=F32)
              + lax.dot_general(wl_ref[...], xh, _NT, preferred_element_type=F32))
    scores = 1.0 / (1.0 + jnp.exp(-logits))
    biased = scores + b_ref[...]

    b3 = biased.reshape(N_GROUPS, GROUP_SIZE, tm)
    eio = lax.broadcasted_iota(jnp.int32, b3.shape, 1)
    m1 = jnp.max(b3, axis=1, keepdims=True)
    i1 = jnp.min(jnp.where(b3 == m1, eio, GROUP_SIZE), axis=1, keepdims=True)
    m2 = jnp.max(jnp.where(eio == i1, -jnp.inf, b3), axis=1, keepdims=True)
    grp = (m1 + m2).reshape(N_GROUPS, tm)

    gio = lax.broadcasted_iota(jnp.int32, grp.shape, 0)
    gsel = jnp.zeros(grp.shape, jnp.bool_)
    for _ in range(TOPK_GROUPS):
        mx = jnp.max(grp, axis=0, keepdims=True)
        ix = jnp.min(jnp.where(grp == mx, gio, N_GROUPS), axis=0, keepdims=True)
        hit = gio == ix
        gsel = gsel | hit
        grp = jnp.where(hit, -jnp.inf, grp)
    emask = jnp.broadcast_to(gsel.reshape(N_GROUPS, 1, tm), b3.shape).reshape(N_EXPERTS, tm)

    masked = jnp.where(emask, biased, -jnp.inf)
    eio64 = lax.broadcasted_iota(jnp.int32, masked.shape, 0)
    hits, sels, idxs = [], [], []
    for _ in range(TOP_K):
        mx = jnp.max(masked, axis=0, keepdims=True)
        ix = jnp.min(jnp.where(masked == mx, eio64, N_EXPERTS), axis=0, keepdims=True)
        hit = eio64 == ix
        hits.append(hit)
        idxs.append(ix)
        sels.append(jnp.sum(jnp.where(hit, scores, 0.0), axis=0, keepdims=True))
        masked = jnp.where(hit, -jnp.inf, masked)
    sel = jnp.concatenate(sels, axis=0)
    gate_ref[...] = sel / jnp.sum(sel, axis=0, keepdims=True) * ROUTED_SCALE
    eidx_ref[...] = jnp.concatenate(idxs, axis=0)

    chosen = jnp.zeros(masked.shape, F32)
    for hit in hits:
        chosen = chosen + jnp.where(hit, 1.0, 0.0)
    r = lax.broadcasted_iota(jnp.int32, (tm, tm), 0)
    c = lax.broadcasted_iota(jnp.int32, (tm, tm), 1)
    earlier = jnp.where(r < c, 1.0, 0.0).astype(BF16)
    base = jnp.dot(chosen.astype(BF16), earlier, preferred_element_type=F32) + run_sc[...]
    ranks = [jnp.sum(jnp.where(hit, base, 0.0), axis=0, keepdims=True) for hit in hits]
    rank_ref[...] = jnp.concatenate(ranks, axis=0).astype(jnp.int32)
    run_sc[...] += jnp.sum(chosen, axis=1, keepdims=True)
    cnt_ref[...] = run_sc[...]


def _router(x32, wh, wl, bias):
    tp, d = x32.shape
    tm = ROW_TILE
    tok = pl.BlockSpec((TOP_K, tm), lambda i: (0, i))
    return pl.pallas_call(
        functools.partial(_router_kernel, tm=tm),
        grid=(tp // tm,),
        in_specs=[pl.BlockSpec((tm, d), lambda i: (i, 0)),
                  pl.BlockSpec((N_EXPERTS, d), lambda i: (0, 0)),
                  pl.BlockSpec((N_EXPERTS, d), lambda i: (0, 0)),
                  pl.BlockSpec((N_EXPERTS, 1), lambda i: (0, 0))],
        out_specs=[tok, tok, tok, pl.BlockSpec((N_EXPERTS, 1), lambda i: (0, 0))],
        out_shape=[jax.ShapeDtypeStruct((TOP_K, tp), jnp.int32), jax.ShapeDtypeStruct((TOP_K, tp), F32),
                   jax.ShapeDtypeStruct((TOP_K, tp), jnp.int32), jax.ShapeDtypeStruct((N_EXPERTS, 1), F32)],
        scratch_shapes=[pltpu.VMEM((N_EXPERTS, 1), F32)],
        compiler_params=_params("arbitrary"),
        name="router",
    )(x32, wh, wl, bias)


def _dispatch_kernel(dest_ref, x_ref, xs_in_ref, xs_ref, sem, *, tm):
    del xs_in_ref
    base = pl.program_id(0) * tm

    def body(t, c):
        for k in range(TOP_K):
            slot = dest_ref[(base + t) * TOP_K + k]
            pltpu.make_async_copy(x_ref.at[pl.ds(t, 1), :], xs_ref.at[pl.ds(slot, 1), :], sem).start()
        return c

    lax.fori_loop(0, tm, body, 0)
    for k in range(TOP_K):
        pltpu.make_async_copy(x_ref, xs_ref.at[pl.ds(0, tm), :], sem).wait()


def _dispatch(dest, x32, n_slots):
    tp, d = x32.shape
    tm = ROW_TILE
    grid_spec = pltpu.PrefetchScalarGridSpec(
        num_scalar_prefetch=1,
        grid=(tp // tm,),
        in_specs=[pl.BlockSpec((tm, d), lambda i, dest: (i, 0)), pl.BlockSpec(memory_space=pl.ANY)],
        out_specs=pl.BlockSpec(memory_space=pl.ANY),
        scratch_shapes=[pltpu.SemaphoreType.DMA(())],
    )
    return pl.pallas_call(
        functools.partial(_dispatch_kernel, tm=tm),
        grid_spec=grid_spec,
        out_shape=jax.ShapeDtypeStruct((n_slots, d), F32),
        input_output_aliases={2: 0},
        compiler_params=_params("arbitrary"),
        name="dispatch",
    )(dest, x32, jnp.zeros((n_slots, d), F32))


def _expert_kernel(be_ref, nused_ref, xs_ref, wg_ref, wu_ref, wd_ref, ys_ref, wg16, wu16, wd16):
    i = pl.program_id(0)

    @pl.when(i < nused_ref[0])
    def _():
        prev = be_ref[jnp.maximum(i - 1, 0)]

        @pl.when((i == 0) | (be_ref[i] != prev))
        def _():
            wg16[...] = wg_ref[...].astype(BF16)
            wu16[...] = wu_ref[...].astype(BF16)
            wd16[...] = wd_ref[...].astype(BF16)

        x = xs_ref[...].astype(BF16)
        g = jnp.dot(x, wg16[...], preferred_element_type=F32)
        u = jnp.dot(x, wu16[...], preferred_element_type=F32)
        h = g * (1.0 / (1.0 + jnp.exp(-g))) * u
        ys_ref[...] = jnp.dot(h.astype(BF16), wd16[...], preferred_element_type=F32)

    @pl.when(i >= nused_ref[0])
    def _():
        ys_ref[...] = jnp.zeros(ys_ref.shape, F32)


def _experts(block_expert, n_used, xs, wg, wu, wd):
    n_slots, d = xs.shape
    f = wg.shape[-1]
    tb = EXPERT_ROWS

    def rows(i, be, nu):
        return (jnp.minimum(i, nu[0] - 1), 0)

    def weights(i, be, nu):
        return (be[jnp.minimum(i, nu[0] - 1)], 0, 0)

    grid_spec = pltpu.PrefetchScalarGridSpec(
        num_scalar_prefetch=2,
        grid=(n_slots // tb,),
        in_specs=[pl.BlockSpec((tb, d), rows),
                  pl.BlockSpec((None, d, f), weights),
                  pl.BlockSpec((None, d, f), weights),
                  pl.BlockSpec((None, f, d), weights)],
        out_specs=pl.BlockSpec((tb, d), lambda i, be, nu: (i, 0)),
        scratch_shapes=[pltpu.VMEM((d, f), BF16), pltpu.VMEM((d, f), BF16), pltpu.VMEM((f, d), BF16)],
    )
    return pl.pallas_call(
        _expert_kernel,
        grid_spec=grid_spec,
        out_shape=jax.ShapeDtypeStruct((n_slots, d), F32),
        compiler_params=_params("arbitrary"),
        name="experts",
    )(block_expert, n_used, xs, wg, wu, wd)


def _combine_kernel(dest_ref, x32_ref, x16_ref, gate_ref, sg_ref, su_ref, sd_ref, g_ref, b_ref, ys_ref,
                    y32_ref, y16_ref, buf, sem, *, tm, alpha):
    base = pl.program_id(0) * tm

    def body(t, c):
        for k in range(TOP_K):
            slot = dest_ref[(base + t) * TOP_K + k]
            pltpu.make_async_copy(ys_ref.at[pl.ds(slot, 1), :], buf.at[k, pl.ds(t, 1), :], sem).start()
        return c

    lax.fori_loop(0, tm, body, 0)

    x16 = x16_ref[...]
    g = jnp.dot(x16, sg_ref[...], preferred_element_type=F32)
    u = jnp.dot(x16, su_ref[...], preferred_element_type=F32)
    h = g * (1.0 / (1.0 + jnp.exp(-g))) * u
    y = jnp.dot(h.astype(BF16), sd_ref[...], preferred_element_type=F32)

    for k in range(TOP_K):
        pltpu.make_async_copy(ys_ref.at[pl.ds(0, tm), :], buf.at[k], sem).wait()
    gate = gate_ref[...]
    for k in range(TOP_K):
        y = y + buf[k] * gate[:, k:k + 1]
    out = _layer_norm(alpha * x32_ref[...] + y, g_ref[...], b_ref[...])
    y32_ref[...] = out
    y16_ref[...] = out.astype(BF16)


def _combine(dest, x32, x16, gate, sg, su, sd, g, b, ys, *, alpha):
    tp, d = x32.shape
    f = sg.shape[-1]
    tm = ROW_TILE
    row = pl.BlockSpec((tm, d), lambda i, dest: (i, 0))
    vec = pl.BlockSpec((1, d), lambda i, dest: (0, 0))
    grid_spec = pltpu.PrefetchScalarGridSpec(
        num_scalar_prefetch=1,
        grid=(tp // tm,),
        in_specs=[row, row,
                  pl.BlockSpec((tm, TOP_K), lambda i, dest: (i, 0)),
                  pl.BlockSpec((d, f), lambda i, dest: (0, 0)),
                  pl.BlockSpec((d, f), lambda i, dest: (0, 0)),
                  pl.BlockSpec((f, d), lambda i, dest: (0, 0)),
                  vec, vec,
                  pl.BlockSpec(memory_space=pl.ANY)],
        out_specs=[row, row],
        scratch_shapes=[pltpu.VMEM((TOP_K, tm, d), F32), pltpu.SemaphoreType.DMA(())],
    )
    return pl.pallas_call(
        functools.partial(_combine_kernel, tm=tm, alpha=alpha),
        grid_spec=grid_spec,
        out_shape=[jax.ShapeDtypeStruct((tp, d), F32), jax.ShapeDtypeStruct((tp, d), BF16)],
        compiler_params=_params("arbitrary"),
        name="combine",
    )(dest, x32, x16, gate, sg, su, sd, g, b, ys)


def _moe(x32, x16, *, w_r, b_r, wg, wu, wd, sg, su, sd, g, b, alpha):
    tp, d = x32.shape
    tb = EXPERT_ROWS
    wh, wl = _split_bf16(w_r.T)
    eidx, gate, rank, cnt = _router(x32, wh, wl, b_r.reshape(N_EXPERTS, 1))

    counts = cnt[:, 0].astype(jnp.int32)
    blocks = (counts + tb - 1) // tb
    blk_end = jnp.cumsum(blocks)
    blk_start = blk_end - blocks
    n_blocks = (tp * TOP_K) // tb + N_EXPERTS
    experts = jnp.arange(N_EXPERTS, dtype=jnp.int32)
    first_slot = jnp.sum(jnp.where(eidx[None] == experts[:, None, None], (blk_start * tb)[:, None, None], 0), axis=0)
    dest = (first_slot + rank).T.reshape(tp * TOP_K)
    block_ids = jnp.arange(n_blocks, dtype=jnp.int32)
    block_expert = jnp.minimum(jnp.sum((blk_end[None, :] <= block_ids[:, None]).astype(jnp.int32), axis=1),
                               N_EXPERTS - 1)
    n_used = blk_end[-1:].astype(jnp.int32)

    xs = _dispatch(dest, x32, n_blocks * tb)
    ys = _experts(block_expert, n_used, xs, wg, wu, wd)
    return _combine(dest, x32, x16, gate.T, sg.astype(BF16), su.astype(BF16), sd.astype(BF16), g, b, ys,
                    alpha=alpha)


def _rope_tables(pos):
    half = DIFF_DK // 2
    inv = ROPE_THETA ** (-jnp.arange(half, dtype=F32) / half)
    ang = pos.astype(F32)[:, None] * inv[None, :]
    cos = jnp.concatenate([jnp.cos(ang)] * 4, axis=-1)
    sin = jnp.concatenate([-jnp.sin(ang), jnp.sin(ang)] * 2, axis=-1)
    return cos, sin


def kernel(x_prompt, x_sample, cache_k, cache_v, page_table, meta_tokens, attn_w_qkv, attn_w_o, diff_lambda, diff_subln, ln_mix_g, ln_mix_b, router_w, router_bias, expert_w_gate, expert_w_up, expert_w_down, shared_w_gate, shared_w_up, shared_w_down, ln_ffn_g, ln_ffn_b):
    bp, seq, d = x_prompt.shape
    nb, dec_seq, _ = x_sample.shape
    depth, _, page, _ = cache_k.shape
    assert bp == 1 and d % (2 * HEAD_DIM) == 0 and 2 * dec_seq <= DECODE_COLS_PER_HEAD
    n_prompt = N_META + seq
    n_dec = nb * dec_seq
    n_tok = n_prompt + n_dec
    tp = -(-n_tok // KV_TILE) * KV_TILE
    past_len = page_table.shape[1] * page
    alpha = (2 * depth) ** 0.25

    x32 = jnp.concatenate([meta_tokens.astype(F32), x_prompt[0], x_sample.reshape(n_dec, d),
                           jnp.zeros((tp - n_tok, d), F32)], axis=0)
    x16 = x32.astype(BF16)
    pos = jnp.concatenate([jnp.arange(n_prompt, dtype=jnp.int32),
                           jnp.tile(past_len + jnp.arange(dec_seq, dtype=jnp.int32), nb),
                           jnp.zeros((tp - n_tok,), jnp.int32)])
    cos, sin = _rope_tables(pos)
    dec = slice(n_prompt, n_tok)
    pad_new = ((0, 0), (0, page - dec_seq), (0, 0))

    k_prompt = jnp.zeros((depth, 1, n_prompt, d), F32)
    v_prompt = jnp.zeros((depth, 1, n_prompt, d), F32)
    ks_rows, vs_rows = [], []
    for i in range(depth):
        diff = i % 2 == 0
        j = i // 2
        lam_init = 0.8 - 0.6 * math.exp(-0.3 * i)
        w3 = attn_w_qkv[i].reshape(d, 3, d).astype(BF16)
        q_scale = DIFF_DK ** -0.5 * LOG2_E if diff else HEAD_DIM ** -0.5
        q16 = _q_proj(x16, w3[:, 0], cos, sin, rope=diff, scale=q_scale)
        k16, k_prompt, k_dec = _kv_proj(x16, w3[:, 1], cos, sin, k_prompt, rope=diff, layer=i, n_prompt=n_prompt,
                                        n_dec=n_dec)
        v16, v_prompt, v_dec = _kv_proj(x16, w3[:, 2], cos, sin, v_prompt, rope=False, layer=i, n_prompt=n_prompt,
                                        n_dec=n_dec)
        lam_vec = diff_lambda[j].astype(F32)
        gain = diff_subln[j].astype(F32)

        o16 = _prompt_attention(q16, k16, v16.T, lam_vec, gain.reshape(HEAD_DIM, 1), diff=diff, lam_init=lam_init)

        qblk = _decode_query_cols(q16[dec].reshape(nb, dec_seq, d), diff=diff, dec_seq=dec_seq)
        k_new = k_dec.reshape(nb, dec_seq, d)
        v_new = v_dec.reshape(nb, dec_seq, d)
        if diff:
            o_dec = _diff_decode(page_table, qblk, jnp.pad(k_new, pad_new), jnp.pad(v_new, pad_new), cache_k,
                                 cache_v, lam_vec, gain.reshape(1, HEAD_DIM), layer=i, lam_init=lam_init,
                                 dec_seq=dec_seq)
        else:
            o_dec = _sb_decode(page_table, qblk, jnp.pad(k_new, pad_new), jnp.pad(v_new, pad_new), cache_k,
                               cache_v, layer=i, dec_seq=dec_seq)
        o16 = lax.dynamic_update_slice(o16, o_dec.reshape(n_dec, d).astype(BF16), (n_prompt, 0))
        ks_rows.append(k_new)
        vs_rows.append(v_new)

        x32, x16 = _oproj_ln(o16, attn_w_o[i].astype(BF16), x32, ln_mix_g[i].reshape(1, d),
                             ln_mix_b[i].reshape(1, d), alpha=alpha)
        x32, x16 = _moe(x32, x16, w_r=router_w[i], b_r=router_bias[i], wg=expert_w_gate[i], wu=expert_w_up[i],
                        wd=expert_w_down[i], sg=shared_w_gate[i], su=shared_w_up[i], sd=shared_w_down[i],
                        g=ln_ffn_g[i].reshape(1, d), b=ln_ffn_b[i].reshape(1, d), alpha=alpha)

    y_prompt = x32[N_META:n_prompt][None]
    y_sample = x32[dec].reshape(nb, dec_seq, d)
    return (y_prompt, y_sample, k_prompt, v_prompt, jnp.stack(ks_rows), jnp.stack(vs_rows))
```

```python
import functools
import math

import jax
import jax.numpy as jnp
from jax import lax
from jax.experimental import pallas as pl
from jax.experimental.pallas import tpu as pltpu

F32 = jnp.float32
BF16 = jnp.bfloat16

N_META = 16
HEAD_DIM = 128
DIFF_DK = 64
ROPE_THETA = 10000.0
N_EXPERTS = 64
N_GROUPS = 8
GROUP_SIZE = N_EXPERTS // N_GROUPS
TOPK_GROUPS = 4
TOP_K = 8
ROUTED_SCALE = 2.5
LN_EPS = 1e-5
NEG_INF = -1e30
LOG2_E = 1.4426950408889634
SB_LOG_FLOOR = -104.0

LANES = 128
ROW_TILE = 256
KV_TILE = 512
EXPERT_ROWS = 256
DECODE_COLS_PER_HEAD = 8
DECODE_PAGES_PER_STEP = 4
VMEM_LIMIT_BYTES = 56 * 1024 * 1024

_NT = (((1,), (1,)), ((), ()))


def _params(*sem):
    return pltpu.CompilerParams(dimension_semantics=sem, vmem_limit_bytes=VMEM_LIMIT_BYTES)


def _softplus(z):
    return jnp.maximum(z, 0.0) + jnp.log(1.0 + jnp.exp(-jnp.abs(z)))


def _layer_norm(y, g, b):
    mu = jnp.mean(y, axis=1, keepdims=True)
    d = y - mu
    var = jnp.mean(d * d, axis=1, keepdims=True)
    return d * lax.rsqrt(var + LN_EPS) * g + b


def _split_bf16(x):
    hi = x.astype(BF16)
    lo = (x - hi.astype(F32)).astype(BF16)
    return hi, lo


def _later_keys(n):
    r = lax.broadcasted_iota(jnp.int32, (n, n), 0)
    c = lax.broadcasted_iota(jnp.int32, (n, n), 1)
    return jnp.where(c > r, 1.0, 0.0).astype(BF16)


def _suffix_sum(later, lk):
    hi, lo = _split_bf16(lk)
    return (jnp.dot(later, hi, preferred_element_type=F32)
            + jnp.dot(later, lo, preferred_element_type=F32))


def _diff_lambda(lam_ref, lam_init):
    lv = lam_ref[...]
    a = jnp.sum(lv[0:1] * lv[1:2], axis=(0, 1), keepdims=True)
    b = jnp.sum(lv[2:3] * lv[3:4], axis=(0, 1), keepdims=True)
    return jnp.exp(a) - jnp.exp(b) + lam_init


def _rope(val, cos, sin):
    lane = lax.broadcasted_iota(jnp.int32, cos.shape, 1)
    first_half = (lane % DIFF_DK) < (DIFF_DK // 2)
    outs = []
    for s in range(val.shape[1] // LANES):
        xs = val[:, s * LANES:(s + 1) * LANES]
        rot = jnp.where(first_half, pltpu.roll(xs, LANES - DIFF_DK // 2, 1), pltpu.roll(xs, DIFF_DK // 2, 1))
        outs.append(xs * cos + rot * sin)
    return jnp.concatenate(outs, axis=1)


def _q_proj_kernel(x_ref, w_ref, cos_ref, sin_ref, q16_ref, *, rope, scale):
    val = jnp.dot(x_ref[...], w_ref[...], preferred_element_type=F32)
    if rope:
        val = _rope(val, cos_ref[...], sin_ref[...])
    q16_ref[...] = (val * scale).astype(BF16)


def _kv_proj_kernel(x_ref, w_ref, cos_ref, sin_ref, all_in_ref, o16_ref, prompt_ref, dec_ref, stage, sem, *,
                    rope, layer, n_prompt, dec_block, dec_off, n_dec):
    del all_in_ref
    i = pl.program_id(0)
    tm = stage.shape[0]
    n_full, rem = divmod(n_prompt, tm)
    last = pl.num_programs(0) - 1
    val = jnp.dot(x_ref[...], w_ref[...], preferred_element_type=F32)
    if rope:
        val = _rope(val, cos_ref[...], sin_ref[...])
    o16_ref[...] = val.astype(BF16)

    def full_copy(blk):
        return pltpu.make_async_copy(stage, prompt_ref.at[layer, 0, pl.ds(blk * tm, tm)], sem)

    def rem_copy():
        return pltpu.make_async_copy(stage.at[pl.ds(0, rem)], prompt_ref.at[layer, 0, pl.ds(n_full * tm, rem)], sem)

    @pl.when((i >= 1) & (i <= n_full))
    def _():
        full_copy(i - 1).wait()

    if rem:
        @pl.when(i == n_full + 1)
        def _():
            rem_copy().wait()

    @pl.when(i < n_full)
    def _():
        stage[...] = val
        full_copy(i).start()

        @pl.when(i == last)
        def _():
            full_copy(i).wait()

    if rem:
        @pl.when(i == n_full)
        def _():
            stage[...] = val
            rem_copy().start()

            @pl.when(i == last)
            def _():
                rem_copy().wait()

    @pl.when(i == dec_block)
    def _():
        dec_ref[...] = val[dec_off:dec_off + n_dec]


def _q_proj(x16, w16, cos, sin, *, rope, scale):
    tp, d = x16.shape
    tm = ROW_TILE
    row = pl.BlockSpec((tm, d), lambda i: (i, 0))
    tab = pl.BlockSpec((tm, LANES), lambda i: (i, 0))
    return pl.pallas_call(
        functools.partial(_q_proj_kernel, rope=rope, scale=scale),
        grid=(tp // tm,),
        in_specs=[row, pl.BlockSpec((d, d), lambda i: (0, 0)), tab, tab],
        out_specs=row,
        out_shape=jax.ShapeDtypeStruct((tp, d), BF16),
        compiler_params=_params("arbitrary"),
        name="q_proj",
    )(x16, w16, cos, sin)


def _kv_proj(x16, w16, cos, sin, all_rows, *, rope, layer, n_prompt, n_dec):
    tp, d = x16.shape
    tm = ROW_TILE
    dec_block, dec_off = divmod(n_prompt, tm)
    assert dec_off + n_dec <= tm and dec_off % 8 == 0, "decode rows must sit inside one row tile"
    row = pl.BlockSpec((tm, d), lambda i: (i, 0))
    tab = pl.BlockSpec((tm, LANES), lambda i: (i, 0))
    return pl.pallas_call(
        functools.partial(_kv_proj_kernel, rope=rope, layer=layer, n_prompt=n_prompt, dec_block=dec_block,
                          dec_off=dec_off, n_dec=n_dec),
        grid=(tp // tm,),
        in_specs=[row, pl.BlockSpec((d, d), lambda i: (0, 0)), tab, tab, pl.BlockSpec(memory_space=pl.ANY)],
        out_specs=[row, pl.BlockSpec(memory_space=pl.ANY), pl.BlockSpec((n_dec, d), lambda i: (0, 0))],
        out_shape=[jax.ShapeDtypeStruct((tp, d), BF16),
                   jax.ShapeDtypeStruct(all_rows.shape, F32),
                   jax.ShapeDtypeStruct((n_dec, d), F32)],
        scratch_shapes=[pltpu.VMEM((tm, d), F32), pltpu.SemaphoreType.DMA(())],
        input_output_aliases={4: 1},
        compiler_params=_params("arbitrary"),
        name="kv_proj",
    )(x16, w16, cos, sin, all_rows)


def _diff_attn_kernel(q_ref, k_ref, vt_ref, lam_ref, gain_ref, o_ref, m_sc, l_sc, acc_sc, *, tq, tk, lam_init):
    qi = pl.program_id(1)
    q = q_ref[...]
    lane = lax.broadcasted_iota(jnp.int32, q.shape, 1)
    zero = jnp.zeros_like(q)
    q2 = jnp.concatenate([jnp.where(lane < DIFF_DK, q, zero), jnp.where(lane >= DIFF_DK, q, zero)], axis=0)
    m_sc[...] = jnp.full(m_sc.shape, NEG_INF, F32)
    l_sc[...] = jnp.zeros(l_sc.shape, F32)
    acc_sc[...] = jnp.zeros(acc_sc.shape, F32)

    def blocks(js, masked):
        starts = [pl.multiple_of(j * tk, tk) for j in js]
        ss = [lax.dot_general(k_ref[pl.ds(st, tk), :], q2, _NT, preferred_element_type=F32)
              for st in starts]
        if masked:
            key = starts[-1] + lax.broadcasted_iota(jnp.int32, ss[-1].shape, 0)
            col = lax.broadcasted_iota(jnp.int32, ss[-1].shape, 1)
            qpos = qi * tq + jnp.where(col >= tq, col - tq, col)
            ss[-1] = jnp.where(key <= qpos, ss[-1], NEG_INF)
        m_prev = m_sc[...]
        m_new = m_prev
        for s in ss:
            m_new = jnp.maximum(m_new, jnp.max(s, axis=0, keepdims=True))
        alpha = jnp.exp2(m_prev - m_new)
        l_new = alpha * l_sc[...]
        acc = alpha * acc_sc[...]
        for st, s in zip(starts, ss):
            p = jnp.exp2(s - m_new)
            l_new = l_new + jnp.sum(p, axis=0, keepdims=True)
            acc = acc + jnp.dot(vt_ref[:, pl.ds(st, tk)], p.astype(BF16), preferred_element_type=F32)
        l_sc[...] = l_new
        acc_sc[...] = acc
        m_sc[...] = m_new

    last = (qi * tq) // tk

    def body(jj, c):
        blocks([2 * jj, 2 * jj + 1], False)
        return c

    lax.fori_loop(0, last // 2, body, 0)

    @pl.when(last % 2 == 1)
    def _():
        blocks([last - 1], False)

    blocks([last], True)

    o = acc_sc[...] / l_sc[...]
    lam = _diff_lambda(lam_ref, lam_init)
    od = o[:, :tq] - lam * o[:, tq:]
    ms = jnp.mean(od * od, axis=0, keepdims=True)
    on = od * lax.rsqrt(ms + LN_EPS) * gain_ref[...] * (1.0 - lam_init)
    o_ref[...] = on.T.astype(BF16)


def _sb_attn_kernel(q_ref, k_ref, vt_ref, o_ref, r_sc, acc_sc, *, tq, tk):
    qi = pl.program_id(1)
    q = q_ref[...]
    later = _later_keys(tk)
    r_sc[...] = jnp.zeros(r_sc.shape, F32)
    acc_sc[...] = jnp.zeros(acc_sc.shape, F32)

    def block(j, masked):
        start = pl.multiple_of(j * tk, tk)
        kb = k_ref[pl.ds(start, tk), :]
        vtb = vt_ref[:, pl.ds(start, tk)]
        z = lax.dot_general(kb, q, _NT, preferred_element_type=F32)
        sp = _softplus(z)
        lk = -sp
        if masked:
            key = start + lax.broadcasted_iota(jnp.int32, z.shape, 0)
            qpos = qi * tq + lax.broadcasted_iota(jnp.int32, z.shape, 1)
            valid = key < qpos
            lk = jnp.where(valid, lk, 0.0)
        r_prev = r_sc[...]
        w = jnp.exp(z - sp + _suffix_sum(later, lk) + r_prev)
        if masked:
            w = jnp.where(valid, w, 0.0)
        acc_sc[...] += jnp.dot(vtb, w.astype(BF16), preferred_element_type=F32)
        r_sc[...] = r_prev + jnp.sum(lk, axis=0, keepdims=True)

    last = (qi * tq) // tk
    block(last, True)

    def cond(j):
        return (j >= 0) & (jnp.max(r_sc[...]) > SB_LOG_FLOOR)

    def body(j):
        block(j, False)
        return j - 1

    lax.while_loop(cond, body, last - 1)
    o_ref[...] = acc_sc[...].T.astype(BF16)


def _prompt_attention(q16, k16, vt16, lam_vec, gain_col, *, diff, lam_init):
    tp, d = q16.shape
    tq, tk = ROW_TILE, KV_TILE
    heads = d // HEAD_DIM
    qspec = pl.BlockSpec((tq, HEAD_DIM), lambda h, i: (i, h))
    kspec = pl.BlockSpec((tp, HEAD_DIM), lambda h, i: (0, h))
    vspec = pl.BlockSpec((HEAD_DIM, tp), lambda h, i: (h, 0))
    ospec = pl.BlockSpec((tq, HEAD_DIM), lambda h, i: (i, h))
    out_shape = jax.ShapeDtypeStruct((tp, d), BF16)
    if diff:
        return pl.pallas_call(
            functools.partial(_diff_attn_kernel, tq=tq, tk=tk, lam_init=lam_init),
            grid=(heads, tp // tq),
            in_specs=[qspec, kspec, vspec,
                      pl.BlockSpec(lam_vec.shape, lambda h, i: (0, 0)),
                      pl.BlockSpec(gain_col.shape, lambda h, i: (0, 0))],
            out_specs=ospec,
            out_shape=out_shape,
            scratch_shapes=[pltpu.VMEM((1, 2 * tq), F32), pltpu.VMEM((1, 2 * tq), F32),
                            pltpu.VMEM((HEAD_DIM, 2 * tq), F32)],
            compiler_params=_params("arbitrary", "arbitrary"),
            name="diff_attention",
        )(q16, k16, vt16, lam_vec, gain_col)
    return pl.pallas_call(
        functools.partial(_sb_attn_kernel, tq=tq, tk=tq),
        grid=(heads, tp // tq),
        in_specs=[qspec, kspec, vspec],
        out_specs=ospec,
        out_shape=out_shape,
        scratch_shapes=[pltpu.VMEM((1, tq), F32), pltpu.VMEM((HEAD_DIM, tq), F32)],
        compiler_params=_params("arbitrary", "arbitrary"),
        name="sb_attention",
    )(q16, k16, vt16)


def _per_row(stat):
    return jnp.broadcast_to(stat, (HEAD_DIM, stat.shape[1])).T


def _decode_pv(acc_sc, ws, v_refs, heads, alpha=None):
    cph = DECODE_COLS_PER_HEAD
    w16s = [w.astype(BF16) for w in ws]
    first = lax.broadcasted_iota(jnp.int32, (2 * cph, HEAD_DIM), 0) < cph
    for g in range(heads // 2):
        rows = slice(2 * g * cph, (2 * g + 2) * cph)
        upd = None
        for w16, v_ref in zip(w16s, v_refs):
            wn = w16[rows]
            lo = jnp.dot(wn, v_ref[:, (2 * g) * HEAD_DIM:(2 * g + 1) * HEAD_DIM].astype(BF16),
                         preferred_element_type=F32)
            hi = jnp.dot(wn, v_ref[:, (2 * g + 1) * HEAD_DIM:(2 * g + 2) * HEAD_DIM].astype(BF16),
                         preferred_element_type=F32)
            part = jnp.where(first, lo, hi)
            upd = part if upd is None else upd + part
        if alpha is None:
            acc_sc[rows, :] += upd
        else:
            acc_sc[rows, :] = alpha[rows] * acc_sc[rows, :] + upd


def _diff_decode_kernel(pt_ref, qb_ref, kn_ref, vn_ref, *rest, n_steps, dec_seq, lam_init, heads):
    n_group = DECODE_PAGES_PER_STEP
    cache_refs = rest[:2 * n_group]
    lam_ref, gain_ref, o_ref, m_sc, l_sc, acc_sc = rest[2 * n_group:]
    s = pl.program_id(1)
    cph = DECODE_COLS_PER_HEAD

    @pl.when(s == 0)
    def _():
        m_sc[...] = jnp.full(m_sc.shape, NEG_INF, F32)
        l_sc[...] = jnp.zeros(l_sc.shape, F32)
        acc_sc[...] = jnp.zeros(acc_sc.shape, F32)

    def step(k_refs, v_refs, is_new):
        qb = qb_ref[...]
        scs = [jnp.dot(k_ref[...].astype(BF16), qb, preferred_element_type=F32) for k_ref in k_refs]
        if is_new:
            key = lax.broadcasted_iota(jnp.int32, scs[0].shape, 0)
            col = lax.broadcasted_iota(jnp.int32, scs[0].shape, 1)
            scs = [jnp.where((key <= (col % cph) % dec_seq) & (key < dec_seq), scs[0], NEG_INF)]
        m_prev = m_sc[...]
        m_new = m_prev
        for sc in scs:
            m_new = jnp.maximum(m_new, jnp.max(sc, axis=0, keepdims=True))
        alpha = jnp.exp2(m_prev - m_new)
        ws = [jnp.exp2(sc - m_new) for sc in scs]
        l_new = alpha * l_sc[...]
        for w in ws:
            l_new = l_new + jnp.sum(w, axis=0, keepdims=True)
        l_sc[...] = l_new
        m_sc[...] = m_new
        _decode_pv(acc_sc, [w.T for w in ws], v_refs, heads, _per_row(alpha))

    @pl.when(s == 0)
    def _():
        step([kn_ref], [vn_ref], True)

    @pl.when(s > 0)
    def _():
        step(list(cache_refs[0::2]), list(cache_refs[1::2]), False)

    @pl.when(s == n_steps - 1)
    def _():
        lam = _diff_lambda(lam_ref, lam_init)
        l_rows = _per_row(l_sc[...])
        for h in range(heads):
            rows = slice(h * cph, (h + 1) * cph)
            on = acc_sc[rows, :] / l_rows[rows]
            od = on[0:dec_seq] - lam * on[dec_seq:2 * dec_seq]
            ms = jnp.mean(od * od, axis=1, keepdims=True)
            o_ref[:, h * HEAD_DIM:(h + 1) * HEAD_DIM] = (od * lax.rsqrt(ms + LN_EPS) * gain_ref[...]
                                                         * (1.0 - lam_init))


def _diff_decode(page_table, qblk, k_new, v_new, cache_k, cache_v, lam_vec, gain_row, *, layer, lam_init, dec_seq):
    nb, n_pages = page_table.shape
    _, _, page, d = cache_k.shape
    heads = d // HEAD_DIM
    cols = heads * DECODE_COLS_PER_HEAD
    n_group = DECODE_PAGES_PER_STEP
    assert n_pages % n_group == 0
    n_steps = n_pages // n_group + 1

    def page_map(which):
        return lambda b, s, pt: (layer, pt[b, n_group * (jnp.maximum(s, 1) - 1) + which], 0, 0)

    per_seq = lambda b, s, pt: (b, 0, 0)
    cache_spec = lambda which: pl.BlockSpec((None, None, page, d), page_map(which))
    cache_specs = [cache_spec(n // 2) for n in range(2 * n_group)]
    grid_spec = pltpu.PrefetchScalarGridSpec(
        num_scalar_prefetch=1,
        grid=(nb, n_steps),
        in_specs=[
            pl.BlockSpec((None, d, cols), per_seq),
            pl.BlockSpec((None, page, d), per_seq),
            pl.BlockSpec((None, page, d), per_seq),
            *cache_specs,
            pl.BlockSpec(lam_vec.shape, lambda b, s, pt: (0, 0)),
            pl.BlockSpec(gain_row.shape, lambda b, s, pt: (0, 0)),
        ],
        out_specs=pl.BlockSpec((None, dec_seq, d), per_seq),
        scratch_shapes=[pltpu.VMEM((1, cols), F32), pltpu.VMEM((1, cols), F32), pltpu.VMEM((cols, HEAD_DIM), F32)],
    )
    return pl.pallas_call(
        functools.partial(_diff_decode_kernel, n_steps=n_steps, dec_seq=dec_seq, lam_init=lam_init, heads=heads),
        grid_spec=grid_spec,
        out_shape=jax.ShapeDtypeStruct((nb, dec_seq, d), F32),
        compiler_params=_params("arbitrary", "arbitrary"),
        name="diff_decode",
    )(page_table, qblk, k_new, v_new, *([cache_k, cache_v] * n_group), lam_vec, gain_row)


def _sb_decode_kernel(pt_ref, qb_ref, kn_ref, vn_ref, kc_hbm, vc_hbm, o_ref, kbuf, vbuf, sems, r_sc, acc_sc, *,
                      layer, n_pages, dec_seq, heads):
    b = pl.program_id(0)
    cph = DECODE_COLS_PER_HEAD
    page = kbuf.shape[1]
    later = _later_keys(page)

    def copies(p, slot):
        pg = pt_ref[b, p]
        return (pltpu.make_async_copy(kc_hbm.at[layer, pg], kbuf.at[slot], sems.at[0, slot]),
                pltpu.make_async_copy(vc_hbm.at[layer, pg], vbuf.at[slot], sems.at[1, slot]))

    def fetch(p, slot):
        for c in copies(p, slot):
            c.start()

    def wait(p, slot):
        for c in copies(p, slot):
            c.wait()

    fetch(n_pages - 1, 0)
    r_sc[...] = jnp.zeros(r_sc.shape, F32)
    acc_sc[...] = jnp.zeros(acc_sc.shape, F32)

    def step(k_ref, v_ref, is_new):
        z = jnp.dot(k_ref[...].astype(BF16), qb_ref[...], preferred_element_type=F32)
        sp = _softplus(z)
        lk = -sp
        if is_new:
            key = lax.broadcasted_iota(jnp.int32, z.shape, 0)
            col = lax.broadcasted_iota(jnp.int32, z.shape, 1)
            valid = key < col % cph
            lk = jnp.where(valid, lk, 0.0)
        r_prev = r_sc[...]
        w = jnp.exp(z - sp + _suffix_sum(later, lk) + r_prev)
        if is_new:
            w = jnp.where(valid, w, 0.0)
        r_sc[...] = r_prev + jnp.sum(lk, axis=0, keepdims=True)
        _decode_pv(acc_sc, [w.T], [v_ref], heads)

    step(kn_ref, vn_ref, True)
    real_col = lax.broadcasted_iota(jnp.int32, r_sc.shape, 1) % cph < dec_seq

    def cond(p):
        return (p >= 0) & (jnp.max(jnp.where(real_col, r_sc[...], NEG_INF)) > SB_LOG_FLOOR)

    def body(p):
        slot = (n_pages - 1 - p) % 2
        wait(p, slot)

        @pl.when(p > 0)
        def _():
            fetch(p - 1, 1 - slot)

        step(kbuf.at[slot], vbuf.at[slot], False)
        return p - 1

    p_end = lax.while_loop(cond, body, n_pages - 1)

    @pl.when(p_end >= 0)
    def _():
        wait(p_end, (n_pages - 1 - p_end) % 2)

    for h in range(heads):
        o_ref[:, h * HEAD_DIM:(h + 1) * HEAD_DIM] = acc_sc[h * cph:h * cph + dec_seq, :]


def _sb_decode(page_table, qblk, k_new, v_new, cache_k, cache_v, *, layer, dec_seq):
    nb, n_pages = page_table.shape
    _, _, page, d = cache_k.shape
    heads = d // HEAD_DIM
    cols = heads * DECODE_COLS_PER_HEAD
    per_seq = lambda b, pt: (b, 0, 0)
    grid_spec = pltpu.PrefetchScalarGridSpec(
        num_scalar_prefetch=1,
        grid=(nb,),
        in_specs=[
            pl.BlockSpec((None, d, cols), per_seq),
            pl.BlockSpec((None, page, d), per_seq),
            pl.BlockSpec((None, page, d), per_seq),
            pl.BlockSpec(memory_space=pl.ANY),
            pl.BlockSpec(memory_space=pl.ANY),
        ],
        out_specs=pl.BlockSpec((None, dec_seq, d), per_seq),
        scratch_shapes=[pltpu.VMEM((2, page, d), F32), pltpu.VMEM((2, page, d), F32),
                        pltpu.SemaphoreType.DMA((2, 2)),
                        pltpu.VMEM((1, cols), F32), pltpu.VMEM((cols, HEAD_DIM), F32)],
    )
    return pl.pallas_call(
        functools.partial(_sb_decode_kernel, layer=layer, n_pages=n_pages, dec_seq=dec_seq, heads=heads),
        grid_spec=grid_spec,
        out_shape=jax.ShapeDtypeStruct((nb, dec_seq, d), F32),
        compiler_params=_params("arbitrary"),
        name="sb_decode",
    )(page_table, qblk, k_new, v_new, cache_k, cache_v)


def _decode_query_cols(q, *, diff, dec_seq):
    nb, _, d = q.shape
    heads = d // HEAD_DIM
    qh = q.reshape(nb, dec_seq, heads, HEAD_DIM).transpose(0, 2, 1, 3)
    if diff:
        lane = jnp.arange(HEAD_DIM) < DIFF_DK
        qh = jnp.concatenate([jnp.where(lane, qh, 0), jnp.where(lane, 0, qh)], axis=2)
    qh = jnp.pad(qh, ((0, 0), (0, 0), (0, DECODE_COLS_PER_HEAD - qh.shape[2]), (0, 0)))
    eye = jnp.eye(heads, dtype=q.dtype)
    blk = qh.transpose(0, 1, 3, 2)[:, :, :, None, :] * eye[None, :, None, :, None]
    return blk.reshape(nb, d, heads * DECODE_COLS_PER_HEAD)


def _oproj_ln_kernel(o_ref, w_ref, x_ref, g_ref, b_ref, y32_ref, y16_ref, *, alpha):
    h = jnp.dot(o_ref[...], w_ref[...], preferred_element_type=F32)
    y = _layer_norm(alpha * x_ref[...] + h, g_ref[...], b_ref[...])
    y32_ref[...] = y
    y16_ref[...] = y.astype(BF16)


def _oproj_ln(o16, w16, x32, g, b, *, alpha):
    tp, d = x32.shape
    tm = ROW_TILE
    row = pl.BlockSpec((tm, d), lambda i: (i, 0))
    vec = pl.BlockSpec((1, d), lambda i: (0, 0))
    return pl.pallas_call(
        functools.partial(_oproj_ln_kernel, alpha=alpha),
        grid=(tp // tm,),
        in_specs=[row, pl.BlockSpec((d, d), lambda i: (0, 0)), row, vec, vec],
        out_specs=[row, row],
        out_shape=[jax.ShapeDtypeStruct((tp, d), F32), jax.ShapeDtypeStruct((tp, d), BF16)],
        compiler_params=_params("arbitrary"),
        name="oproj_ln",
    )(o16, w16, x32, g, b)


def _router_kernel(x_ref, wh_ref, wl_ref, b_ref, eidx_ref, gate_ref, rank_ref, cnt_ref, run_sc, *, tm):
    i = pl.program_id(0)

    @pl.when(i == 0)
    def _():
        run_sc[...] = jnp.zeros(run_sc.shape, F32)

    xh, xl = _split_bf16(x_ref[...])
    wh = wh_ref[...]
    logits = (lax.dot_general(wh, xh, _NT, preferred_element_type=F32)
              + lax.dot_general(wh, xl, _NT, preferred_element_type=F32)
              + lax.dot_general(wl_ref[...], xh, _NT, preferred_element_type=F32))
    scores = 1.0 / (1.0 + jnp.exp(-logits))
    biased = scores + b_ref[...]

    b3 = biased.reshape(N_GROUPS, GROUP_SIZE, tm)
    eio = lax.broadcasted_iota(jnp.int32, b3.shape, 1)
    m1 = jnp.max(b3, axis=1, keepdims=True)
    i1 = jnp.min(jnp.where(b3 == m1, eio, GROUP_SIZE), axis=1, keepdims=True)
    m2 = jnp.max(jnp.where(eio == i1, -jnp.inf, b3), axis=1, keepdims=True)
    grp = (m1 + m2).reshape(N_GROUPS, tm)

    gio = lax.broadcasted_iota(jnp.int32, grp.shape, 0)
    gsel = jnp.zeros(grp.shape, jnp.bool_)
    for _ in range(TOPK_GROUPS):
        mx = jnp.max(grp, axis=0, keepdims=True)
        ix = jnp.min(jnp.where(grp == mx, gio, N_GROUPS), axis=0, keepdims=True)
        hit = gio == ix
        gsel = gsel | hit
        grp = jnp.where(hit, -jnp.inf, grp)
    emask = jnp.broadcast_to(gsel.reshape(N_GROUPS, 1, tm), b3.shape).reshape(N_EXPERTS, tm)

    masked = jnp.where(emask, biased, -jnp.inf)
    eio64 = lax.broadcasted_iota(jnp.int32, masked.shape, 0)
    hits, sels, idxs = [], [], []
    for _ in range(TOP_K):
        mx = jnp.max(masked, axis=0, keepdims=True)
        ix = jnp.min(jnp.where(masked == mx, eio64, N_EXPERTS), axis=0, keepdims=True)
        hit = eio64 == ix
        hits.append(hit)
        idxs.append(ix)
        sels.append(jnp.sum(jnp.where(hit, scores, 0.0), axis=0, keepdims=True))
        masked = jnp.where(hit, -jnp.inf, masked)
    sel = jnp.concatenate(sels, axis=0)
    gate_ref[...] = sel / jnp.sum(sel, axis=0, keepdims=True) * ROUTED_SCALE
    eidx_ref[...] = jnp.concatenate(idxs, axis=0)

    chosen = jnp.zeros(masked.shape, F32)
    for hit in hits:
        chosen = chosen + jnp.where(hit, 1.0, 0.0)
    r = lax.broadcasted_iota(jnp.int32, (tm, tm), 0)
    c = lax.broadcasted_iota(jnp.int32, (tm, tm), 1)
    earlier = jnp.where(r < c, 1.0, 0.0).astype(BF16)
    base = jnp.dot(chosen.astype(BF16), earlier, preferred_element_type=F32) + run_sc[...]
    ranks = [jnp.sum(jnp.where(hit, base, 0.0), axis=0, keepdims=True) for hit in hits]
    rank_ref[...] = jnp.concatenate(ranks, axis=0).astype(jnp.int32)
    run_sc[...] += jnp.sum(chosen, axis=1, keepdims=True)
    cnt_ref[...] = run_sc[...]


def _router(x32, wh, wl, bias):
    tp, d = x32.shape
    tm = ROW_TILE
    tok = pl.BlockSpec((TOP_K, tm), lambda i: (0, i))
    return pl.pallas_call(
        functools.partial(_router_kernel, tm=tm),
        grid=(tp // tm,),
        in_specs=[pl.BlockSpec((tm, d), lambda i: (i, 0)),
                  pl.BlockSpec((N_EXPERTS, d), lambda i: (0, 0)),
                  pl.BlockSpec((N_EXPERTS, d), lambda i: (0, 0)),
                  pl.BlockSpec((N_EXPERTS, 1), lambda i: (0, 0))],
        out_specs=[tok, tok, tok, pl.BlockSpec((N_EXPERTS, 1), lambda i: (0, 0))],
        out_shape=[jax.ShapeDtypeStruct((TOP_K, tp), jnp.int32), jax.ShapeDtypeStruct((TOP_K, tp), F32),
                   jax.ShapeDtypeStruct((TOP_K, tp), jnp.int32), jax.ShapeDtypeStruct((N_EXPERTS, 1), F32)],
        scratch_shapes=[pltpu.VMEM((N_EXPERTS, 1), F32)],
        compiler_params=_params("arbitrary"),
        name="router",
    )(x32, wh, wl, bias)


def _dispatch_kernel(dest_ref, x_ref, xs_in_ref, xs_ref, sem, *, tm):
    del xs_in_ref
    base = pl.program_id(0) * tm

    def body(t, c):
        for k in range(TOP_K):
            slot = dest_ref[(base + t) * TOP_K + k]
            pltpu.make_async_copy(x_ref.at[pl.ds(t, 1), :], xs_ref.at[pl.ds(slot, 1), :], sem).start()
        return c

    lax.fori_loop(0, tm, body, 0)
    for k in range(TOP_K):
        pltpu.make_async_copy(x_ref, xs_ref.at[pl.ds(0, tm), :], sem).wait()


def _dispatch(dest, x32, n_slots):
    tp, d = x32.shape
    tm = ROW_TILE
    grid_spec = pltpu.PrefetchScalarGridSpec(
        num_scalar_prefetch=1,
        grid=(tp // tm,),
        in_specs=[pl.BlockSpec((tm, d), lambda i, dest: (i, 0)), pl.BlockSpec(memory_space=pl.ANY)],
        out_specs=pl.BlockSpec(memory_space=pl.ANY),
        scratch_shapes=[pltpu.SemaphoreType.DMA(())],
    )
    return pl.pallas_call(
        functools.partial(_dispatch_kernel, tm=tm),
        grid_spec=grid_spec,
        out_shape=jax.ShapeDtypeStruct((n_slots, d), F32),
        input_output_aliases={2: 0},
        compiler_params=_params("arbitrary"),
        name="dispatch",
    )(dest, x32, jnp.zeros((n_slots, d), F32))


def _expert_kernel(be_ref, nused_ref, xs_ref, wg_ref, wu_ref, wd_ref, ys_ref, wg16, wu16, wd16):
    i = pl.program_id(0)

    @pl.when(i < nused_ref[0])
    def _():
        prev = be_ref[jnp.maximum(i - 1, 0)]

        @pl.when((i == 0) | (be_ref[i] != prev))
        def _():
            wg16[...] = wg_ref[...].astype(BF16)
            wu16[...] = wu_ref[...].astype(BF16)
            wd16[...] = wd_ref[...].astype(BF16)

        x = xs_ref[...].astype(BF16)
        g = jnp.dot(x, wg16[...], preferred_element_type=F32)
        u = jnp.dot(x, wu16[...], preferred_element_type=F32)
        h = g * (1.0 / (1.0 + jnp.exp(-g))) * u
        ys_ref[...] = jnp.dot(h.astype(BF16), wd16[...], preferred_element_type=F32)

    @pl.when(i >= nused_ref[0])
    def _():
        ys_ref[...] = jnp.zeros(ys_ref.shape, F32)


def _experts(block_expert, n_used, xs, wg, wu, wd, *, layer):
    n_slots, d = xs.shape
    f = wg.shape[-1]
    tb = EXPERT_ROWS

    def rows(i, be, nu):
        return (jnp.minimum(i, nu[0] - 1), 0)

    def weights(i, be, nu):
        return (layer, be[jnp.minimum(i, nu[0] - 1)], 0, 0)

    grid_spec = pltpu.PrefetchScalarGridSpec(
        num_scalar_prefetch=2,
        grid=(n_slots // tb,),
        in_specs=[pl.BlockSpec((tb, d), rows),
                  pl.BlockSpec((None, None, d, f), weights),
                  pl.BlockSpec((None, None, d, f), weights),
                  pl.BlockSpec((None, None, f, d), weights)],
        out_specs=pl.BlockSpec((tb, d), lambda i, be, nu: (i, 0)),
        scratch_shapes=[pltpu.VMEM((d, f), BF16), pltpu.VMEM((d, f), BF16), pltpu.VMEM((f, d), BF16)],
    )
    return pl.pallas_call(
        _expert_kernel,
        grid_spec=grid_spec,
        out_shape=jax.ShapeDtypeStruct((n_slots, d), F32),
        compiler_params=_params("arbitrary"),
        name="experts",
    )(block_expert, n_used, xs, wg, wu, wd)


def _combine_kernel(dest_ref, x32_ref, x16_ref, gate_ref, sg_ref, su_ref, sd_ref, g_ref, b_ref, ys_ref,
                    y32_ref, y16_ref, buf, sem, *, tm, alpha):
    base = pl.program_id(0) * tm

    def body(t, c):
        for k in range(TOP_K):
            slot = dest_ref[(base + t) * TOP_K + k]
            pltpu.make_async_copy(ys_ref.at[pl.ds(slot, 1), :], buf.at[k, pl.ds(t, 1), :], sem).start()
        return c

    lax.fori_loop(0, tm, body, 0)

    x16 = x16_ref[...]
    g = jnp.dot(x16, sg_ref[...], preferred_element_type=F32)
    u = jnp.dot(x16, su_ref[...], preferred_element_type=F32)
    h = g * (1.0 / (1.0 + jnp.exp(-g))) * u
    y = jnp.dot(h.astype(BF16), sd_ref[...], preferred_element_type=F32)

    for k in range(TOP_K):
        pltpu.make_async_copy(ys_ref.at[pl.ds(0, tm), :], buf.at[k], sem).wait()
    gate = gate_ref[...]
    for k in range(TOP_K):
        y = y + buf[k] * gate[:, k:k + 1]
    out = _layer_norm(alpha * x32_ref[...] + y, g_ref[...], b_ref[...])
    y32_ref[...] = out
    y16_ref[...] = out.astype(BF16)


def _combine(dest, x32, x16, gate, sg, su, sd, g, b, ys, *, alpha):
    tp, d = x32.shape
    f = sg.shape[-1]
    tm = ROW_TILE
    row = pl.BlockSpec((tm, d), lambda i, dest: (i, 0))
    vec = pl.BlockSpec((1, d), lambda i, dest: (0, 0))
    grid_spec = pltpu.PrefetchScalarGridSpec(
        num_scalar_prefetch=1,
        grid=(tp // tm,),
        in_specs=[row, row,
                  pl.BlockSpec((tm, TOP_K), lambda i, dest: (i, 0)),
                  pl.BlockSpec((d, f), lambda i, dest: (0, 0)),
                  pl.BlockSpec((d, f), lambda i, dest: (0, 0)),
                  pl.BlockSpec((f, d), lambda i, dest: (0, 0)),
                  vec, vec,
                  pl.BlockSpec(memory_space=pl.ANY)],
        out_specs=[row, row],
        scratch_shapes=[pltpu.VMEM((TOP_K, tm, d), F32), pltpu.SemaphoreType.DMA(())],
    )
    return pl.pallas_call(
        functools.partial(_combine_kernel, tm=tm, alpha=alpha),
        grid_spec=grid_spec,
        out_shape=[jax.ShapeDtypeStruct((tp, d), F32), jax.ShapeDtypeStruct((tp, d), BF16)],
        compiler_params=_params("arbitrary"),
        name="combine",
    )(dest, x32, x16, gate, sg, su, sd, g, b, ys)


def _moe(x32, x16, *, layer, w_r, b_r, wg, wu, wd, sg, su, sd, g, b, alpha):
    tp, d = x32.shape
    tb = EXPERT_ROWS
    wh, wl = _split_bf16(w_r.T)
    eidx, gate, rank, cnt = _router(x32, wh, wl, b_r.reshape(N_EXPERTS, 1))

    counts = cnt[:, 0].astype(jnp.int32)
    blocks = (counts + tb - 1) // tb
    blk_end = jnp.cumsum(blocks)
    blk_start = blk_end - blocks
    n_blocks = (tp * TOP_K) // tb + N_EXPERTS
    experts = jnp.arange(N_EXPERTS, dtype=jnp.int32)
    first_slot = jnp.sum(jnp.where(eidx[None] == experts[:, None, None], (blk_start * tb)[:, None, None], 0), axis=0)
    dest = (first_slot + rank).T.reshape(tp * TOP_K)
    block_ids = jnp.arange(n_blocks, dtype=jnp.int32)
    block_expert = jnp.minimum(jnp.sum((blk_end[None, :] <= block_ids[:, None]).astype(jnp.int32), axis=1),
                               N_EXPERTS - 1)
    n_used = blk_end[-1:].astype(jnp.int32)

    xs = _dispatch(dest, x32, n_blocks * tb)
    ys = _experts(block_expert, n_used, xs, wg, wu, wd, layer=layer)
    return _combine(dest, x32, x16, gate.T, sg.astype(BF16), su.astype(BF16), sd.astype(BF16), g, b, ys,
                    alpha=alpha)


def _rope_tables(pos):
    half = DIFF_DK // 2
    inv = ROPE_THETA ** (-jnp.arange(half, dtype=F32) / half)
    ang = pos.astype(F32)[:, None] * inv[None, :]
    cos = jnp.concatenate([jnp.cos(ang)] * 4, axis=-1)
    sin = jnp.concatenate([-jnp.sin(ang), jnp.sin(ang)] * 2, axis=-1)
    return cos, sin


def kernel(x_prompt, x_sample, cache_k, cache_v, page_table, meta_tokens, attn_w_qkv, attn_w_o, diff_lambda, diff_subln, ln_mix_g, ln_mix_b, router_w, router_bias, expert_w_gate, expert_w_up, expert_w_down, shared_w_gate, shared_w_up, shared_w_down, ln_ffn_g, ln_ffn_b):
    bp, seq, d = x_prompt.shape
    nb, dec_seq, _ = x_sample.shape
    depth, _, page, _ = cache_k.shape
    assert bp == 1 and d % (2 * HEAD_DIM) == 0 and 2 * dec_seq <= DECODE_COLS_PER_HEAD
    n_prompt = N_META + seq
    n_dec = nb * dec_seq
    n_tok = n_prompt + n_dec
    tp = -(-n_tok // KV_TILE) * KV_TILE
    past_len = page_table.shape[1] * page
    alpha = (2 * depth) ** 0.25

    x32 = jnp.concatenate([meta_tokens.astype(F32), x_prompt[0], x_sample.reshape(n_dec, d),
                           jnp.zeros((tp - n_tok, d), F32)], axis=0)
    x16 = x32.astype(BF16)
    pos = jnp.concatenate([jnp.arange(n_prompt, dtype=jnp.int32),
                           jnp.tile(past_len + jnp.arange(dec_seq, dtype=jnp.int32), nb),
                           jnp.zeros((tp - n_tok,), jnp.int32)])
    cos, sin = _rope_tables(pos)
    dec = slice(n_prompt, n_tok)
    pad_new = ((0, 0), (0, page - dec_seq), (0, 0))

    k_prompt = jnp.zeros((depth, 1, n_prompt, d), F32)
    v_prompt = jnp.zeros((depth, 1, n_prompt, d), F32)
    ks_rows, vs_rows = [], []
    for i in range(depth):
        diff = i % 2 == 0
        j = i // 2
        lam_init = 0.8 - 0.6 * math.exp(-0.3 * i)
        w3 = attn_w_qkv[i].reshape(d, 3, d).astype(BF16)
        q_scale = DIFF_DK ** -0.5 * LOG2_E if diff else HEAD_DIM ** -0.5
        q16 = _q_proj(x16, w3[:, 0], cos, sin, rope=diff, scale=q_scale)
        k16, k_prompt, k_dec = _kv_proj(x16, w3[:, 1], cos, sin, k_prompt, rope=diff, layer=i, n_prompt=n_prompt,
                                        n_dec=n_dec)
        v16, v_prompt, v_dec = _kv_proj(x16, w3[:, 2], cos, sin, v_prompt, rope=False, layer=i, n_prompt=n_prompt,
                                        n_dec=n_dec)
        lam_vec = diff_lambda[j].astype(F32)
        gain = diff_subln[j].astype(F32)

        o16 = _prompt_attention(q16, k16, v16.T, lam_vec, gain.reshape(HEAD_DIM, 1), diff=diff, lam_init=lam_init)

        qblk = _decode_query_cols(q16[dec].reshape(nb, dec_seq, d), diff=diff, dec_seq=dec_seq)
        k_new = k_dec.reshape(nb, dec_seq, d)
        v_new = v_dec.reshape(nb, dec_seq, d)
        if diff:
            o_dec = _diff_decode(page_table, qblk, jnp.pad(k_new, pad_new), jnp.pad(v_new, pad_new), cache_k,
                                 cache_v, lam_vec, gain.reshape(1, HEAD_DIM), layer=i, lam_init=lam_init,
                                 dec_seq=dec_seq)
        else:
            o_dec = _sb_decode(page_table, qblk, jnp.pad(k_new, pad_new), jnp.pad(v_new, pad_new), cache_k,
                               cache_v, layer=i, dec_seq=dec_seq)
        o16 = lax.dynamic_update_slice(o16, o_dec.reshape(n_dec, d).astype(BF16), (n_prompt, 0))
        ks_rows.append(k_new)
        vs_rows.append(v_new)

        x32, x16 = _oproj_ln(o16, attn_w_o[i].astype(BF16), x32, ln_mix_g[i].reshape(1, d),
                             ln_mix_b[i].reshape(1, d), alpha=alpha)
        x32, x16 = _moe(x32, x16, layer=i, w_r=router_w[i], b_r=router_bias[i], wg=expert_w_gate, wu=expert_w_up,
                        wd=expert_w_down, sg=shared_w_gate[i], su=shared_w_up[i], sd=shared_w_down[i],
                        g=ln_ffn_g[i].reshape(1, d), b=ln_ffn_b[i].reshape(1, d), alpha=alpha)

    y_prompt = x32[N_META:n_prompt][None]
    y_sample = x32[dec].reshape(nb, dec_seq, d)
    return (y_prompt, y_sample, k_prompt, v_prompt, jnp.stack(ks_rows), jnp.stack(vs_rows))
```

```python
import functools
import math

import jax
import jax.numpy as jnp
from jax import lax
from jax.experimental import pallas as pl
from jax.experimental.pallas import tpu as pltpu

F32 = jnp.float32
BF16 = jnp.bfloat16

N_META = 16
HEAD_DIM = 128
DIFF_DK = 64
ROPE_THETA = 10000.0
N_EXPERTS = 64
N_GROUPS = 8
GROUP_SIZE = N_EXPERTS // N_GROUPS
TOPK_GROUPS = 4
TOP_K = 8
ROUTED_SCALE = 2.5
LN_EPS = 1e-5
NEG_INF = -1e30
LOG2_E = 1.4426950408889634
SB_LOG_FLOOR = -104.0

LANES = 128
ROW_TILE = 256
KV_TILE = 512
EXPERT_ROWS = 512
DECODE_COLS_PER_HEAD = 8
DECODE_PAGES_PER_STEP = 4
VMEM_LIMIT_BYTES = 56 * 1024 * 1024

_NT = (((1,), (1,)), ((), ()))


def _params(*sem):
    return pltpu.CompilerParams(dimension_semantics=sem, vmem_limit_bytes=VMEM_LIMIT_BYTES)


def _softplus(z):
    return jnp.maximum(z, 0.0) + jnp.log(1.0 + jnp.exp(-jnp.abs(z)))


def _layer_norm(y, g, b):
    mu = jnp.mean(y, axis=1, keepdims=True)
    d = y - mu
    var = jnp.mean(d * d, axis=1, keepdims=True)
    return d * lax.rsqrt(var + LN_EPS) * g + b


def _split_bf16(x):
    hi = x.astype(BF16)
    lo = (x - hi.astype(F32)).astype(BF16)
    return hi, lo


def _later_keys(n):
    r = lax.broadcasted_iota(jnp.int32, (n, n), 0)
    c = lax.broadcasted_iota(jnp.int32, (n, n), 1)
    return jnp.where(c > r, 1.0, 0.0).astype(BF16)


def _suffix_sum(later, lk):
    hi, lo = _split_bf16(lk)
    return (jnp.dot(later, hi, preferred_element_type=F32)
            + jnp.dot(later, lo, preferred_element_type=F32))


def _diff_lambda(lam_ref, lam_init):
    lv = lam_ref[...]
    a = jnp.sum(lv[0:1] * lv[1:2], axis=(0, 1), keepdims=True)
    b = jnp.sum(lv[2:3] * lv[3:4], axis=(0, 1), keepdims=True)
    return jnp.exp(a) - jnp.exp(b) + lam_init


def _rope(val, cos, sin):
    lane = lax.broadcasted_iota(jnp.int32, cos.shape, 1)
    first_half = (lane % DIFF_DK) < (DIFF_DK // 2)
    outs = []
    for s in range(val.shape[1] // LANES):
        xs = val[:, s * LANES:(s + 1) * LANES]
        rot = jnp.where(first_half, pltpu.roll(xs, LANES - DIFF_DK // 2, 1), pltpu.roll(xs, DIFF_DK // 2, 1))
        outs.append(xs * cos + rot * sin)
    return jnp.concatenate(outs, axis=1)


def _q_proj_kernel(x_ref, w_ref, cos_ref, sin_ref, q16_ref, *, rope, scale):
    val = jnp.dot(x_ref[...], w_ref[...], preferred_element_type=F32)
    if rope:
        val = _rope(val, cos_ref[...], sin_ref[...])
    q16_ref[...] = (val * scale).astype(BF16)


def _kv_proj_kernel(x_ref, w_ref, cos_ref, sin_ref, all_in_ref, o16_ref, prompt_ref, dec_ref, stage, sem, *,
                    rope, layer, n_prompt, dec_block, dec_off, n_dec):
    del all_in_ref
    i = pl.program_id(0)
    tm = stage.shape[0]
    n_full, rem = divmod(n_prompt, tm)
    last = pl.num_programs(0) - 1
    val = jnp.dot(x_ref[...], w_ref[...], preferred_element_type=F32)
    if rope:
        val = _rope(val, cos_ref[...], sin_ref[...])
    o16_ref[...] = val.astype(BF16)

    def full_copy(blk):
        return pltpu.make_async_copy(stage, prompt_ref.at[layer, 0, pl.ds(blk * tm, tm)], sem)

    def rem_copy():
        return pltpu.make_async_copy(stage.at[pl.ds(0, rem)], prompt_ref.at[layer, 0, pl.ds(n_full * tm, rem)], sem)

    @pl.when((i >= 1) & (i <= n_full))
    def _():
        full_copy(i - 1).wait()

    if rem:
        @pl.when(i == n_full + 1)
        def _():
            rem_copy().wait()

    @pl.when(i < n_full)
    def _():
        stage[...] = val
        full_copy(i).start()

        @pl.when(i == last)
        def _():
            full_copy(i).wait()

    if rem:
        @pl.when(i == n_full)
        def _():
            stage[...] = val
            rem_copy().start()

            @pl.when(i == last)
            def _():
                rem_copy().wait()

    @pl.when(i == dec_block)
    def _():
        dec_ref[...] = val[dec_off:dec_off + n_dec]


def _q_proj(x16, w16, cos, sin, *, rope, scale):
    tp, d = x16.shape
    tm = ROW_TILE
    row = pl.BlockSpec((tm, d), lambda i: (i, 0))
    tab = pl.BlockSpec((tm, LANES), lambda i: (i, 0))
    return pl.pallas_call(
        functools.partial(_q_proj_kernel, rope=rope, scale=scale),
        grid=(tp // tm,),
        in_specs=[row, pl.BlockSpec((d, d), lambda i: (0, 0)), tab, tab],
        out_specs=row,
        out_shape=jax.ShapeDtypeStruct((tp, d), BF16),
        compiler_params=_params("arbitrary"),
        name="q_proj",
    )(x16, w16, cos, sin)


def _kv_proj(x16, w16, cos, sin, all_rows, *, rope, layer, n_prompt, n_dec):
    tp, d = x16.shape
    tm = ROW_TILE
    dec_block, dec_off = divmod(n_prompt, tm)
    assert dec_off + n_dec <= tm and dec_off % 8 == 0, "decode rows must sit inside one row tile"
    row = pl.BlockSpec((tm, d), lambda i: (i, 0))
    tab = pl.BlockSpec((tm, LANES), lambda i: (i, 0))
    return pl.pallas_call(
        functools.partial(_kv_proj_kernel, rope=rope, layer=layer, n_prompt=n_prompt, dec_block=dec_block,
                          dec_off=dec_off, n_dec=n_dec),
        grid=(tp // tm,),
        in_specs=[row, pl.BlockSpec((d, d), lambda i: (0, 0)), tab, tab, pl.BlockSpec(memory_space=pl.ANY)],
        out_specs=[row, pl.BlockSpec(memory_space=pl.ANY), pl.BlockSpec((n_dec, d), lambda i: (0, 0))],
        out_shape=[jax.ShapeDtypeStruct((tp, d), BF16),
                   jax.ShapeDtypeStruct(all_rows.shape, F32),
                   jax.ShapeDtypeStruct((n_dec, d), F32)],
        scratch_shapes=[pltpu.VMEM((tm, d), F32), pltpu.SemaphoreType.DMA(())],
        input_output_aliases={4: 1},
        compiler_params=_params("arbitrary"),
        name="kv_proj",
    )(x16, w16, cos, sin, all_rows)


def _diff_attn_kernel(q_ref, k_ref, vt_ref, lam_ref, gain_ref, o_ref, m_sc, l_sc, acc_sc, *, tq, tk, lam_init):
    qi = pl.program_id(1)
    q = q_ref[...]
    lane = lax.broadcasted_iota(jnp.int32, q.shape, 1)
    zero = jnp.zeros_like(q)
    q2 = jnp.concatenate([jnp.where(lane < DIFF_DK, q, zero), jnp.where(lane >= DIFF_DK, q, zero)], axis=0)
    m_sc[...] = jnp.full(m_sc.shape, NEG_INF, F32)
    l_sc[...] = jnp.zeros(l_sc.shape, F32)
    acc_sc[...] = jnp.zeros(acc_sc.shape, F32)

    def blocks(js, masked):
        starts = [pl.multiple_of(j * tk, tk) for j in js]
        ss = [lax.dot_general(k_ref[pl.ds(st, tk), :], q2, _NT, preferred_element_type=F32)
              for st in starts]
        if masked:
            key = starts[-1] + lax.broadcasted_iota(jnp.int32, ss[-1].shape, 0)
            col = lax.broadcasted_iota(jnp.int32, ss[-1].shape, 1)
            qpos = qi * tq + jnp.where(col >= tq, col - tq, col)
            ss[-1] = jnp.where(key <= qpos, ss[-1], NEG_INF)
        m_prev = m_sc[...]
        m_new = m_prev
        for s in ss:
            m_new = jnp.maximum(m_new, jnp.max(s, axis=0, keepdims=True))
        alpha = jnp.exp2(m_prev - m_new)
        l_new = alpha * l_sc[...]
        acc = alpha * acc_sc[...]
        for st, s in zip(starts, ss):
            p = jnp.exp2(s - m_new)
            l_new = l_new + jnp.sum(p, axis=0, keepdims=True)
            acc = acc + jnp.dot(vt_ref[:, pl.ds(st, tk)], p.astype(BF16), preferred_element_type=F32)
        l_sc[...] = l_new
        acc_sc[...] = acc
        m_sc[...] = m_new

    last = (qi * tq) // tk

    def body(jj, c):
        blocks([4 * jj, 4 * jj + 1, 4 * jj + 2, 4 * jj + 3], False)
        return c

    lax.fori_loop(0, last // 4, body, 0)
    done = (last // 4) * 4

    @pl.when(last - done >= 2)
    def _():
        blocks([done, done + 1], False)

    @pl.when((last - done) % 2 == 1)
    def _():
        blocks([last - 1], False)

    blocks([last], True)

    o = acc_sc[...] / l_sc[...]
    lam = _diff_lambda(lam_ref, lam_init)
    od = o[:, :tq] - lam * o[:, tq:]
    ms = jnp.mean(od * od, axis=0, keepdims=True)
    on = od * lax.rsqrt(ms + LN_EPS) * gain_ref[...] * (1.0 - lam_init)
    o_ref[...] = on.T.astype(BF16)


def _sb_attn_kernel(q_ref, k_ref, vt_ref, o_ref, r_sc, acc_sc, *, tq, tk):
    qi = pl.program_id(1)
    q = q_ref[...]
    later = _later_keys(tk)
    r_sc[...] = jnp.zeros(r_sc.shape, F32)
    acc_sc[...] = jnp.zeros(acc_sc.shape, F32)

    def block(j, masked):
        start = pl.multiple_of(j * tk, tk)
        kb = k_ref[pl.ds(start, tk), :]
        vtb = vt_ref[:, pl.ds(start, tk)]
        z = lax.dot_general(kb, q, _NT, preferred_element_type=F32)
        sp = _softplus(z)
        lk = -sp
        if masked:
            key = start + lax.broadcasted_iota(jnp.int32, z.shape, 0)
            qpos = qi * tq + lax.broadcasted_iota(jnp.int32, z.shape, 1)
            valid = key < qpos
            lk = jnp.where(valid, lk, 0.0)
        r_prev = r_sc[...]
        w = jnp.exp(z - sp + _suffix_sum(later, lk) + r_prev)
        if masked:
            w = jnp.where(valid, w, 0.0)
        acc_sc[...] += jnp.dot(vtb, w.astype(BF16), preferred_element_type=F32)
        r_sc[...] = r_prev + jnp.sum(lk, axis=0, keepdims=True)

    last = (qi * tq) // tk
    block(last, True)

    def cond(j):
        return (j >= 0) & (jnp.max(r_sc[...]) > SB_LOG_FLOOR)

    def body(j):
        block(j, False)
        return j - 1

    lax.while_loop(cond, body, last - 1)
    o_ref[...] = acc_sc[...].T.astype(BF16)


def _prompt_attention(q16, k16, vt16, lam_vec, gain_col, *, diff, lam_init):
    tp, d = q16.shape
    tq, tk = ROW_TILE, KV_TILE
    heads = d // HEAD_DIM
    qspec = pl.BlockSpec((tq, HEAD_DIM), lambda h, i: (i, h))
    kspec = pl.BlockSpec((tp, HEAD_DIM), lambda h, i: (0, h))
    vspec = pl.BlockSpec((HEAD_DIM, tp), lambda h, i: (h, 0))
    ospec = pl.BlockSpec((tq, HEAD_DIM), lambda h, i: (i, h))
    out_shape = jax.ShapeDtypeStruct((tp, d), BF16)
    if diff:
        return pl.pallas_call(
            functools.partial(_diff_attn_kernel, tq=tq, tk=tk, lam_init=lam_init),
            grid=(heads, tp // tq),
            in_specs=[qspec, kspec, vspec,
                      pl.BlockSpec(lam_vec.shape, lambda h, i: (0, 0)),
                      pl.BlockSpec(gain_col.shape, lambda h, i: (0, 0))],
            out_specs=ospec,
            out_shape=out_shape,
            scratch_shapes=[pltpu.VMEM((1, 2 * tq), F32), pltpu.VMEM((1, 2 * tq), F32),
                            pltpu.VMEM((HEAD_DIM, 2 * tq), F32)],
            compiler_params=_params("arbitrary", "arbitrary"),
            name="diff_attention",
        )(q16, k16, vt16, lam_vec, gain_col)
    return pl.pallas_call(
        functools.partial(_sb_attn_kernel, tq=tq, tk=tq),
        grid=(heads, tp // tq),
        in_specs=[qspec, kspec, vspec],
        out_specs=ospec,
        out_shape=out_shape,
        scratch_shapes=[pltpu.VMEM((1, tq), F32), pltpu.VMEM((HEAD_DIM, tq), F32)],
        compiler_params=_params("arbitrary", "arbitrary"),
        name="sb_attention",
    )(q16, k16, vt16)


def _per_row(stat):
    return jnp.broadcast_to(stat, (HEAD_DIM, stat.shape[1])).T


def _decode_pv(acc_sc, ws, v_refs, heads, alpha=None):
    cph = DECODE_COLS_PER_HEAD
    w16s = [w.astype(BF16) for w in ws]
    first = lax.broadcasted_iota(jnp.int32, (2 * cph, HEAD_DIM), 0) < cph
    for g in range(heads // 2):
        rows = slice(2 * g * cph, (2 * g + 2) * cph)
        upd = None
        for w16, v_ref in zip(w16s, v_refs):
            wn = w16[rows]
            lo = jnp.dot(wn, v_ref[:, (2 * g) * HEAD_DIM:(2 * g + 1) * HEAD_DIM].astype(BF16),
                         preferred_element_type=F32)
            hi = jnp.dot(wn, v_ref[:, (2 * g + 1) * HEAD_DIM:(2 * g + 2) * HEAD_DIM].astype(BF16),
                         preferred_element_type=F32)
            part = jnp.where(first, lo, hi)
            upd = part if upd is None else upd + part
        if alpha is None:
            acc_sc[rows, :] += upd
        else:
            acc_sc[rows, :] = alpha[rows] * acc_sc[rows, :] + upd


def _diff_decode_kernel(pt_ref, qb_ref, kn_ref, vn_ref, *rest, n_steps, dec_seq, lam_init, heads):
    n_group = DECODE_PAGES_PER_STEP
    cache_refs = rest[:2 * n_group]
    lam_ref, gain_ref, o_ref, m_sc, l_sc, acc_sc = rest[2 * n_group:]
    s = pl.program_id(1)
    cph = DECODE_COLS_PER_HEAD

    @pl.when(s == 0)
    def _():
        m_sc[...] = jnp.full(m_sc.shape, NEG_INF, F32)
        l_sc[...] = jnp.zeros(l_sc.shape, F32)
        acc_sc[...] = jnp.zeros(acc_sc.shape, F32)

    def step(k_refs, v_refs, is_new):
        qb = qb_ref[...]
        scs = [jnp.dot(k_ref[...].astype(BF16), qb, preferred_element_type=F32) for k_ref in k_refs]
        if is_new:
            key = lax.broadcasted_iota(jnp.int32, scs[0].shape, 0)
            col = lax.broadcasted_iota(jnp.int32, scs[0].shape, 1)
            scs = [jnp.where((key <= (col % cph) % dec_seq) & (key < dec_seq), scs[0], NEG_INF)]
        m_prev = m_sc[...]
        m_new = m_prev
        for sc in scs:
            m_new = jnp.maximum(m_new, jnp.max(sc, axis=0, keepdims=True))
        alpha = jnp.exp2(m_prev - m_new)
        ws = [jnp.exp2(sc - m_new) for sc in scs]
        l_new = alpha * l_sc[...]
        for w in ws:
            l_new = l_new + jnp.sum(w, axis=0, keepdims=True)
        l_sc[...] = l_new
        m_sc[...] = m_new
        _decode_pv(acc_sc, [w.T for w in ws], v_refs, heads, _per_row(alpha))

    @pl.when(s == 0)
    def _():
        step([kn_ref], [vn_ref], True)

    @pl.when(s > 0)
    def _():
        step(list(cache_refs[0::2]), list(cache_refs[1::2]), False)

    @pl.when(s == n_steps - 1)
    def _():
        lam = _diff_lambda(lam_ref, lam_init)
        l_rows = _per_row(l_sc[...])
        for h in range(heads):
            rows = slice(h * cph, (h + 1) * cph)
            on = acc_sc[rows, :] / l_rows[rows]
            od = on[0:dec_seq] - lam * on[dec_seq:2 * dec_seq]
            ms = jnp.mean(od * od, axis=1, keepdims=True)
            o_ref[:, h * HEAD_DIM:(h + 1) * HEAD_DIM] = (od * lax.rsqrt(ms + LN_EPS) * gain_ref[...]
                                                         * (1.0 - lam_init))


def _diff_decode(page_table, qblk, k_new, v_new, cache_k, cache_v, lam_vec, gain_row, *, layer, lam_init, dec_seq):
    nb, n_pages = page_table.shape
    _, _, page, d = cache_k.shape
    heads = d // HEAD_DIM
    cols = heads * DECODE_COLS_PER_HEAD
    n_group = DECODE_PAGES_PER_STEP
    assert n_pages % n_group == 0
    n_steps = n_pages // n_group + 1

    def page_map(which):
        return lambda b, s, pt: (layer, pt[b, n_group * (jnp.maximum(s, 1) - 1) + which], 0, 0)

    per_seq = lambda b, s, pt: (b, 0, 0)
    cache_spec = lambda which: pl.BlockSpec((None, None, page, d), page_map(which))
    cache_specs = [cache_spec(n // 2) for n in range(2 * n_group)]
    grid_spec = pltpu.PrefetchScalarGridSpec(
        num_scalar_prefetch=1,
        grid=(nb, n_steps),
        in_specs=[
            pl.BlockSpec((None, d, cols), per_seq),
            pl.BlockSpec((None, page, d), per_seq),
            pl.BlockSpec((None, page, d), per_seq),
            *cache_specs,
            pl.BlockSpec(lam_vec.shape, lambda b, s, pt: (0, 0)),
            pl.BlockSpec(gain_row.shape, lambda b, s, pt: (0, 0)),
        ],
        out_specs=pl.BlockSpec((None, dec_seq, d), per_seq),
        scratch_shapes=[pltpu.VMEM((1, cols), F32), pltpu.VMEM((1, cols), F32), pltpu.VMEM((cols, HEAD_DIM), F32)],
    )
    return pl.pallas_call(
        functools.partial(_diff_decode_kernel, n_steps=n_steps, dec_seq=dec_seq, lam_init=lam_init, heads=heads),
        grid_spec=grid_spec,
        out_shape=jax.ShapeDtypeStruct((nb, dec_seq, d), F32),
        compiler_params=_params("arbitrary", "arbitrary"),
        name="diff_decode",
    )(page_table, qblk, k_new, v_new, *([cache_k, cache_v] * n_group), lam_vec, gain_row)


def _sb_decode_kernel(pt_ref, qb_ref, kn_ref, vn_ref, kc_hbm, vc_hbm, o_ref, kbuf, vbuf, sems, r_sc, acc_sc, *,
                      layer, n_pages, dec_seq, heads):
    b = pl.program_id(0)
    cph = DECODE_COLS_PER_HEAD
    page = kbuf.shape[1]
    later = _later_keys(page)

    def copies(p, slot):
        pg = pt_ref[b, p]
        return (pltpu.make_async_copy(kc_hbm.at[layer, pg], kbuf.at[slot], sems.at[0, slot]),
                pltpu.make_async_copy(vc_hbm.at[layer, pg], vbuf.at[slot], sems.at[1, slot]))

    def fetch(p, slot):
        for c in copies(p, slot):
            c.start()

    def wait(p, slot):
        for c in copies(p, slot):
            c.wait()

    fetch(n_pages - 1, 0)
    r_sc[...] = jnp.zeros(r_sc.shape, F32)
    acc_sc[...] = jnp.zeros(acc_sc.shape, F32)

    def step(k_ref, v_ref, is_new):
        z = jnp.dot(k_ref[...].astype(BF16), qb_ref[...], preferred_element_type=F32)
        sp = _softplus(z)
        lk = -sp
        if is_new:
            key = lax.broadcasted_iota(jnp.int32, z.shape, 0)
            col = lax.broadcasted_iota(jnp.int32, z.shape, 1)
            valid = key < col % cph
            lk = jnp.where(valid, lk, 0.0)
        r_prev = r_sc[...]
        w = jnp.exp(z - sp + _suffix_sum(later, lk) + r_prev)
        if is_new:
            w = jnp.where(valid, w, 0.0)
        r_sc[...] = r_prev + jnp.sum(lk, axis=0, keepdims=True)
        _decode_pv(acc_sc, [w.T], [v_ref], heads)

    step(kn_ref, vn_ref, True)
    real_col = lax.broadcasted_iota(jnp.int32, r_sc.shape, 1) % cph < dec_seq

    def cond(p):
        return (p >= 0) & (jnp.max(jnp.where(real_col, r_sc[...], NEG_INF)) > SB_LOG_FLOOR)

    def body(p):
        slot = (n_pages - 1 - p) % 2
        wait(p, slot)

        @pl.when(p > 0)
        def _():
            fetch(p - 1, 1 - slot)

        step(kbuf.at[slot], vbuf.at[slot], False)
        return p - 1

    p_end = lax.while_loop(cond, body, n_pages - 1)

    @pl.when(p_end >= 0)
    def _():
        wait(p_end, (n_pages - 1 - p_end) % 2)

    for h in range(heads):
        o_ref[:, h * HEAD_DIM:(h + 1) * HEAD_DIM] = acc_sc[h * cph:h * cph + dec_seq, :]


def _sb_decode(page_table, qblk, k_new, v_new, cache_k, cache_v, *, layer, dec_seq):
    nb, n_pages = page_table.shape
    _, _, page, d = cache_k.shape
    heads = d // HEAD_DIM
    cols = heads * DECODE_COLS_PER_HEAD
    per_seq = lambda b, pt: (b, 0, 0)
    grid_spec = pltpu.PrefetchScalarGridSpec(
        num_scalar_prefetch=1,
        grid=(nb,),
        in_specs=[
            pl.BlockSpec((None, d, cols), per_seq),
            pl.BlockSpec((None, page, d), per_seq),
            pl.BlockSpec((None, page, d), per_seq),
            pl.BlockSpec(memory_space=pl.ANY),
            pl.BlockSpec(memory_space=pl.ANY),
        ],
        out_specs=pl.BlockSpec((None, dec_seq, d), per_seq),
        scratch_shapes=[pltpu.VMEM((2, page, d), F32), pltpu.VMEM((2, page, d), F32),
                        pltpu.SemaphoreType.DMA((2, 2)),
                        pltpu.VMEM((1, cols), F32), pltpu.VMEM((cols, HEAD_DIM), F32)],
    )
    return pl.pallas_call(
        functools.partial(_sb_decode_kernel, layer=layer, n_pages=n_pages, dec_seq=dec_seq, heads=heads),
        grid_spec=grid_spec,
        out_shape=jax.ShapeDtypeStruct((nb, dec_seq, d), F32),
        compiler_params=_params("arbitrary"),
        name="sb_decode",
    )(page_table, qblk, k_new, v_new, cache_k, cache_v)


def _decode_query_cols(q, *, diff, dec_seq):
    nb, _, d = q.shape
    heads = d // HEAD_DIM
    qh = q.reshape(nb, dec_seq, heads, HEAD_DIM).transpose(0, 2, 1, 3)
    if diff:
        lane = jnp.arange(HEAD_DIM) < DIFF_DK
        qh = jnp.concatenate([jnp.where(lane, qh, 0), jnp.where(lane, 0, qh)], axis=2)
    qh = jnp.pad(qh, ((0, 0), (0, 0), (0, DECODE_COLS_PER_HEAD - qh.shape[2]), (0, 0)))
    eye = jnp.eye(heads, dtype=q.dtype)
    blk = qh.transpose(0, 1, 3, 2)[:, :, :, None, :] * eye[None, :, None, :, None]
    return blk.reshape(nb, d, heads * DECODE_COLS_PER_HEAD)


def _pack_bf16_pairs(y):
    h = y.shape[1] // 2
    bits = pltpu.bitcast(y.astype(BF16).astype(F32), jnp.uint32)
    return (bits[:, :h] >> 16) | bits[:, h:]


def _unpack_bf16_pairs(packed):
    lo = pltpu.bitcast(packed << 16, F32)
    hi = pltpu.bitcast(packed & jnp.uint32(0xFFFF0000), F32)
    return lo, hi


def _oproj_ln_kernel(o_ref, w_ref, x_ref, g_ref, b_ref, y32_ref, y16_ref, ypk_ref, *, alpha):
    h = jnp.dot(o_ref[...], w_ref[...], preferred_element_type=F32)
    y = _layer_norm(alpha * x_ref[...] + h, g_ref[...], b_ref[...])
    y32_ref[...] = y
    y16_ref[...] = y.astype(BF16)
    ypk_ref[...] = _pack_bf16_pairs(y)


def _oproj_ln(o16, w16, x32, g, b, *, alpha):
    tp, d = x32.shape
    tm = ROW_TILE
    row = pl.BlockSpec((tm, d), lambda i: (i, 0))
    vec = pl.BlockSpec((1, d), lambda i: (0, 0))
    return pl.pallas_call(
        functools.partial(_oproj_ln_kernel, alpha=alpha),
        grid=(tp // tm,),
        in_specs=[row, pl.BlockSpec((d, d), lambda i: (0, 0)), row, vec, vec],
        out_specs=[row, row, pl.BlockSpec((tm, d // 2), lambda i: (i, 0))],
        out_shape=[jax.ShapeDtypeStruct((tp, d), F32), jax.ShapeDtypeStruct((tp, d), BF16),
                   jax.ShapeDtypeStruct((tp, d // 2), jnp.uint32)],
        compiler_params=_params("arbitrary"),
        name="oproj_ln",
    )(o16, w16, x32, g, b)


def _router_kernel(x_ref, wh_ref, wl_ref, b_ref, eidx_ref, gate_ref, rank_ref, cnt_ref, run_sc, *, tm):
    i = pl.program_id(0)

    @pl.when(i == 0)
    def _():
        run_sc[...] = jnp.zeros(run_sc.shape, F32)

    xh, xl = _split_bf16(x_ref[...])
    wh = wh_ref[...]
    logits = (lax.dot_general(wh, xh, _NT, preferred_element_type=F32)
              + lax.dot_general(wh, xl, _NT, preferred_element_type=F32)
              + lax.dot_general(wl_ref[...], xh, _NT, preferred_element_type=F32))
    scores = 1.0 / (1.0 + jnp.exp(-logits))
    biased = scores + b_ref[...]

    b3 = biased.reshape(N_GROUPS, GROUP_SIZE, tm)
    eio = lax.broadcasted_iota(jnp.int32, b3.shape, 1)
    m1 = jnp.max(b3, axis=1, keepdims=True)
    i1 = jnp.min(jnp.where(b3 == m1, eio, GROUP_SIZE), axis=1, keepdims=True)
    m2 = jnp.max(jnp.where(eio == i1, -jnp.inf, b3), axis=1, keepdims=True)
    grp = (m1 + m2).reshape(N_GROUPS, tm)

    gio = lax.broadcasted_iota(jnp.int32, grp.shape, 0)
    gsel = jnp.zeros(grp.shape, jnp.bool_)
    for _ in range(TOPK_GROUPS):
        mx = jnp.max(grp, axis=0, keepdims=True)
        ix = jnp.min(jnp.where(grp == mx, gio, N_GROUPS), axis=0, keepdims=True)
        hit = gio == ix
        gsel = gsel | hit
        grp = jnp.where(hit, -jnp.inf, grp)
    emask = jnp.broadcast_to(gsel.reshape(N_GROUPS, 1, tm), b3.shape).reshape(N_EXPERTS, tm)

    masked = jnp.where(emask, biased, -jnp.inf)
    eio64 = lax.broadcasted_iota(jnp.int32, masked.shape, 0)
    hits, sels, idxs = [], [], []
    for _ in range(TOP_K):
        mx = jnp.max(masked, axis=0, keepdims=True)
        ix = jnp.min(jnp.where(masked == mx, eio64, N_EXPERTS), axis=0, keepdims=True)
        hit = eio64 == ix
        hits.append(hit)
        idxs.append(ix)
        sels.append(jnp.sum(jnp.where(hit, scores, 0.0), axis=0, keepdims=True))
        masked = jnp.where(hit, -jnp.inf, masked)
    sel = jnp.concatenate(sels, axis=0)
    gate_ref[...] = sel / jnp.sum(sel, axis=0, keepdims=True) * ROUTED_SCALE
    eidx_ref[...] = jnp.concatenate(idxs, axis=0)

    chosen = jnp.zeros(masked.shape, F32)
    for hit in hits:
        chosen = chosen + jnp.where(hit, 1.0, 0.0)
    r = lax.broadcasted_iota(jnp.int32, (tm, tm), 0)
    c = lax.broadcasted_iota(jnp.int32, (tm, tm), 1)
    earlier = jnp.where(r < c, 1.0, 0.0).astype(BF16)
    base = jnp.dot(chosen.astype(BF16), earlier, preferred_element_type=F32) + run_sc[...]
    ranks = [jnp.sum(jnp.where(hit, base, 0.0), axis=0, keepdims=True) for hit in hits]
    rank_ref[...] = jnp.concatenate(ranks, axis=0).astype(jnp.int32)
    run_sc[...] += jnp.sum(chosen, axis=1, keepdims=True)
    cnt_ref[...] = run_sc[...]


def _router(x32, wh, wl, bias):
    tp, d = x32.shape
    tm = ROW_TILE
    tok = pl.BlockSpec((TOP_K, tm), lambda i: (0, i))
    return pl.pallas_call(
        functools.partial(_router_kernel, tm=tm),
        grid=(tp // tm,),
        in_specs=[pl.BlockSpec((tm, d), lambda i: (i, 0)),
                  pl.BlockSpec((N_EXPERTS, d), lambda i: (0, 0)),
                  pl.BlockSpec((N_EXPERTS, d), lambda i: (0, 0)),
                  pl.BlockSpec((N_EXPERTS, 1), lambda i: (0, 0))],
        out_specs=[tok, tok, tok, pl.BlockSpec((N_EXPERTS, 1), lambda i: (0, 0))],
        out_shape=[jax.ShapeDtypeStruct((TOP_K, tp), jnp.int32), jax.ShapeDtypeStruct((TOP_K, tp), F32),
                   jax.ShapeDtypeStruct((TOP_K, tp), jnp.int32), jax.ShapeDtypeStruct((N_EXPERTS, 1), F32)],
        scratch_shapes=[pltpu.VMEM((N_EXPERTS, 1), F32)],
        compiler_params=_params("arbitrary"),
        name="router",
    )(x32, wh, wl, bias)


def _dispatch_kernel(dest_ref, x_ref, xs_in_ref, xs_ref, sem, *, tm):
    del xs_in_ref
    base = pl.program_id(0) * tm

    def body(t, c):
        for k in range(TOP_K):
            slot = dest_ref[(base + t) * TOP_K + k]
            pltpu.make_async_copy(x_ref.at[pl.ds(t, 1), :], xs_ref.at[pl.ds(slot, 1), :], sem).start()
        return c

    lax.fori_loop(0, tm, body, 0)
    for k in range(TOP_K):
        pltpu.make_async_copy(x_ref, xs_ref.at[pl.ds(0, tm), :], sem).wait()


def _dispatch(dest, rows, n_slots):
    tp, d = rows.shape
    tm = ROW_TILE
    grid_spec = pltpu.PrefetchScalarGridSpec(
        num_scalar_prefetch=1,
        grid=(tp // tm,),
        in_specs=[pl.BlockSpec((tm, d), lambda i, dest: (i, 0)), pl.BlockSpec(memory_space=pl.ANY)],
        out_specs=pl.BlockSpec(memory_space=pl.ANY),
        scratch_shapes=[pltpu.SemaphoreType.DMA(())],
    )
    return pl.pallas_call(
        functools.partial(_dispatch_kernel, tm=tm),
        grid_spec=grid_spec,
        out_shape=jax.ShapeDtypeStruct((n_slots, d), rows.dtype),
        input_output_aliases={2: 0},
        compiler_params=_params("arbitrary"),
        name="dispatch",
    )(dest, rows, jnp.zeros((n_slots, d), rows.dtype))


def _expert_kernel(be_ref, nused_ref, xs_ref, wg_ref, wu_ref, wd_ref, ys_ref, wg16, wu16, wd16):
    i = pl.program_id(0)

    @pl.when(i < nused_ref[0])
    def _():
        prev = be_ref[jnp.maximum(i - 1, 0)]

        @pl.when((i == 0) | (be_ref[i] != prev))
        def _():
            wg16[...] = wg_ref[...].astype(BF16)
            wu16[...] = wu_ref[...].astype(BF16)
            wd16[...] = wd_ref[...].astype(BF16)

        lo, hi = _unpack_bf16_pairs(xs_ref[...])
        lo, hi = lo.astype(BF16), hi.astype(BF16)
        half = lo.shape[1]
        g = (jnp.dot(lo, wg16[:half, :], preferred_element_type=F32)
             + jnp.dot(hi, wg16[half:, :], preferred_element_type=F32))
        u = (jnp.dot(lo, wu16[:half, :], preferred_element_type=F32)
             + jnp.dot(hi, wu16[half:, :], preferred_element_type=F32))
        h = g * (1.0 / (1.0 + jnp.exp(-g))) * u
        ys_ref[...] = _pack_bf16_pairs(jnp.dot(h.astype(BF16), wd16[...], preferred_element_type=F32))

    @pl.when(i >= nused_ref[0])
    def _():
        ys_ref[...] = jnp.zeros(ys_ref.shape, ys_ref.dtype)


def _experts(block_expert, n_used, xs, wg, wu, wd, *, layer):
    n_slots, half = xs.shape
    d = 2 * half
    f = wg.shape[-1]
    tb = EXPERT_ROWS

    def rows(i, be, nu):
        return (jnp.minimum(i, nu[0] - 1), 0)

    def weights(i, be, nu):
        return (layer, be[jnp.minimum(i, nu[0] - 1)], 0, 0)

    grid_spec = pltpu.PrefetchScalarGridSpec(
        num_scalar_prefetch=2,
        grid=(n_slots // tb,),
        in_specs=[pl.BlockSpec((tb, half), rows),
                  pl.BlockSpec((None, None, d, f), weights),
                  pl.BlockSpec((None, None, d, f), weights),
                  pl.BlockSpec((None, None, f, d), weights)],
        out_specs=pl.BlockSpec((tb, half), lambda i, be, nu: (i, 0)),
        scratch_shapes=[pltpu.VMEM((d, f), BF16), pltpu.VMEM((d, f), BF16), pltpu.VMEM((f, d), BF16)],
    )
    return pl.pallas_call(
        _expert_kernel,
        grid_spec=grid_spec,
        out_shape=jax.ShapeDtypeStruct((n_slots, half), jnp.uint32),
        compiler_params=_params("arbitrary"),
        name="experts",
    )(block_expert, n_used, xs, wg, wu, wd)


def _combine_kernel(dest_ref, x32_ref, x16_ref, gate_ref, sg_ref, su_ref, sd_ref, g_ref, b_ref, ys_ref,
                    y32_ref, y16_ref, buf, sem, *, tm, alpha):
    base = pl.program_id(0) * tm

    def body(t, c):
        for k in range(TOP_K):
            slot = dest_ref[(base + t) * TOP_K + k]
            pltpu.make_async_copy(ys_ref.at[pl.ds(slot, 1), :], buf.at[k, pl.ds(t, 1), :], sem).start()
        return c

    lax.fori_loop(0, tm, body, 0)

    x16 = x16_ref[...]
    g = jnp.dot(x16, sg_ref[...], preferred_element_type=F32)
    u = jnp.dot(x16, su_ref[...], preferred_element_type=F32)
    h = g * (1.0 / (1.0 + jnp.exp(-g))) * u
    y = jnp.dot(h.astype(BF16), sd_ref[...], preferred_element_type=F32)

    for k in range(TOP_K):
        pltpu.make_async_copy(ys_ref.at[pl.ds(0, tm), :], buf.at[k], sem).wait()
    gate = gate_ref[...]
    r_lo = r_hi = None
    for k in range(TOP_K):
        lo, hi = _unpack_bf16_pairs(buf[k])
        gk = gate[:, k:k + 1]
        r_lo = lo * gk if r_lo is None else r_lo + lo * gk
        r_hi = hi * gk if r_hi is None else r_hi + hi * gk
    y = y + jnp.concatenate([r_lo, r_hi], axis=1)
    out = _layer_norm(alpha * x32_ref[...] + y, g_ref[...], b_ref[...])
    y32_ref[...] = out
    y16_ref[...] = out.astype(BF16)


def _combine(dest, x32, x16, gate, sg, su, sd, g, b, ys, *, alpha):
    tp, d = x32.shape
    f = sg.shape[-1]
    tm = ROW_TILE
    row = pl.BlockSpec((tm, d), lambda i, dest: (i, 0))
    vec = pl.BlockSpec((1, d), lambda i, dest: (0, 0))
    grid_spec = pltpu.PrefetchScalarGridSpec(
        num_scalar_prefetch=1,
        grid=(tp // tm,),
        in_specs=[row, row,
                  pl.BlockSpec((tm, TOP_K), lambda i, dest: (i, 0)),
                  pl.BlockSpec((d, f), lambda i, dest: (0, 0)),
                  pl.BlockSpec((d, f), lambda i, dest: (0, 0)),
                  pl.BlockSpec((f, d), lambda i, dest: (0, 0)),
                  vec, vec,
                  pl.BlockSpec(memory_space=pl.ANY)],
        out_specs=[row, row],
        scratch_shapes=[pltpu.VMEM((TOP_K, tm, d // 2), jnp.uint32), pltpu.SemaphoreType.DMA(())],
    )
    return pl.pallas_call(
        functools.partial(_combine_kernel, tm=tm, alpha=alpha),
        grid_spec=grid_spec,
        out_shape=[jax.ShapeDtypeStruct((tp, d), F32), jax.ShapeDtypeStruct((tp, d), BF16)],
        compiler_params=_params("arbitrary"),
        name="combine",
    )(dest, x32, x16, gate, sg, su, sd, g, b, ys)


def _moe(x32, x16, xpk, *, layer, w_r, b_r, wg, wu, wd, sg, su, sd, g, b, alpha):
    tp, d = x32.shape
    tb = EXPERT_ROWS
    wh, wl = _split_bf16(w_r.T)
    eidx, gate, rank, cnt = _router(x32, wh, wl, b_r.reshape(N_EXPERTS, 1))

    counts = cnt[:, 0].astype(jnp.int32)
    blocks = (counts + tb - 1) // tb
    blk_end = jnp.cumsum(blocks)
    blk_start = blk_end - blocks
    n_blocks = (tp * TOP_K) // tb + N_EXPERTS
    experts = jnp.arange(N_EXPERTS, dtype=jnp.int32)
    first_slot = jnp.sum(jnp.where(eidx[None] == experts[:, None, None], (blk_start * tb)[:, None, None], 0), axis=0)
    dest = (first_slot + rank).T.reshape(tp * TOP_K)
    block_ids = jnp.arange(n_blocks, dtype=jnp.int32)
    block_expert = jnp.minimum(jnp.sum((blk_end[None, :] <= block_ids[:, None]).astype(jnp.int32), axis=1),
                               N_EXPERTS - 1)
    n_used = blk_end[-1:].astype(jnp.int32)

    xs = _dispatch(dest, xpk, n_blocks * tb)
    ys = _experts(block_expert, n_used, xs, wg, wu, wd, layer=layer)
    return _combine(dest, x32, x16, gate.T, sg.astype(BF16), su.astype(BF16), sd.astype(BF16), g, b, ys,
                    alpha=alpha)


def _rope_tables(pos):
    half = DIFF_DK // 2
    inv = ROPE_THETA ** (-jnp.arange(half, dtype=F32) / half)
    ang = pos.astype(F32)[:, None] * inv[None, :]
    cos = jnp.concatenate([jnp.cos(ang)] * 4, axis=-1)
    sin = jnp.concatenate([-jnp.sin(ang), jnp.sin(ang)] * 2, axis=-1)
    return cos, sin


def kernel(x_prompt, x_sample, cache_k, cache_v, page_table, meta_tokens, attn_w_qkv, attn_w_o, diff_lambda, diff_subln, ln_mix_g, ln_mix_b, router_w, router_bias, expert_w_gate, expert_w_up, expert_w_down, shared_w_gate, shared_w_up, shared_w_down, ln_ffn_g, ln_ffn_b):
    bp, seq, d = x_prompt.shape
    nb, dec_seq, _ = x_sample.shape
    depth, _, page, _ = cache_k.shape
    assert bp == 1 and d % (2 * HEAD_DIM) == 0 and 2 * dec_seq <= DECODE_COLS_PER_HEAD
    n_prompt = N_META + seq
    n_dec = nb * dec_seq
    n_tok = n_prompt + n_dec
    tp = -(-n_tok // KV_TILE) * KV_TILE
    past_len = page_table.shape[1] * page
    alpha = (2 * depth) ** 0.25

    x32 = jnp.concatenate([meta_tokens.astype(F32), x_prompt[0], x_sample.reshape(n_dec, d),
                           jnp.zeros((tp - n_tok, d), F32)], axis=0)
    x16 = x32.astype(BF16)
    pos = jnp.concatenate([jnp.arange(n_prompt, dtype=jnp.int32),
                           jnp.tile(past_len + jnp.arange(dec_seq, dtype=jnp.int32), nb),
                           jnp.zeros((tp - n_tok,), jnp.int32)])
    cos, sin = _rope_tables(pos)
    dec = slice(n_prompt, n_tok)
    pad_new = ((0, 0), (0, page - dec_seq), (0, 0))

    k_prompt = jnp.zeros((depth, 1, n_prompt, d), F32)
    v_prompt = jnp.zeros((depth, 1, n_prompt, d), F32)
    ks_rows, vs_rows = [], []
    for i in range(depth):
        diff = i % 2 == 0
        j = i // 2
        lam_init = 0.8 - 0.6 * math.exp(-0.3 * i)
        w3 = attn_w_qkv[i].reshape(d, 3, d).astype(BF16)
        q_scale = DIFF_DK ** -0.5 * LOG2_E if diff else HEAD_DIM ** -0.5
        q16 = _q_proj(x16, w3[:, 0], cos, sin, rope=diff, scale=q_scale)
        k16, k_prompt, k_dec = _kv_proj(x16, w3[:, 1], cos, sin, k_prompt, rope=diff, layer=i, n_prompt=n_prompt,
                                        n_dec=n_dec)
        v16, v_prompt, v_dec = _kv_proj(x16, w3[:, 2], cos, sin, v_prompt, rope=False, layer=i, n_prompt=n_prompt,
                                        n_dec=n_dec)
        lam_vec = diff_lambda[j].astype(F32)
        gain = diff_subln[j].astype(F32)

        o16 = _prompt_attention(q16, k16, v16.T, lam_vec, gain.reshape(HEAD_DIM, 1), diff=diff, lam_init=lam_init)

        qblk = _decode_query_cols(q16[dec].reshape(nb, dec_seq, d), diff=diff, dec_seq=dec_seq)
        k_new = k_dec.reshape(nb, dec_seq, d)
        v_new = v_dec.reshape(nb, dec_seq, d)
        if diff:
            o_dec = _diff_decode(page_table, qblk, jnp.pad(k_new, pad_new), jnp.pad(v_new, pad_new), cache_k,
                                 cache_v, lam_vec, gain.reshape(1, HEAD_DIM), layer=i, lam_init=lam_init,
                                 dec_seq=dec_seq)
        else:
            o_dec = _sb_decode(page_table, qblk, jnp.pad(k_new, pad_new), jnp.pad(v_new, pad_new), cache_k,
                               cache_v, layer=i, dec_seq=dec_seq)
        o16 = lax.dynamic_update_slice(o16, o_dec.reshape(n_dec, d).astype(BF16), (n_prompt, 0))
        ks_rows.append(k_new)
        vs_rows.append(v_new)

        x32, x16, xpk = _oproj_ln(o16, attn_w_o[i].astype(BF16), x32, ln_mix_g[i].reshape(1, d),
                             ln_mix_b[i].reshape(1, d), alpha=alpha)
        x32, x16 = _moe(x32, x16, xpk, layer=i, w_r=router_w[i], b_r=router_bias[i], wg=expert_w_gate, wu=expert_w_up,
                        wd=expert_w_down, sg=shared_w_gate[i], su=shared_w_up[i], sd=shared_w_down[i],
                        g=ln_ffn_g[i].reshape(1, d), b=ln_ffn_b[i].reshape(1, d), alpha=alpha)

    y_prompt = x32[N_META:n_prompt][None]
    y_sample = x32[dec].reshape(nb, dec_seq, d)
    return (y_prompt, y_sample, k_prompt, v_prompt, jnp.stack(ks_rows), jnp.stack(vs_rows))
```
